```python
import math
import jax
import jax.numpy as jnp
from jax import lax
import numpy as np

D_MODEL = 1024
BATCH = 4
SEQ = 8192
DEPTH = 2

CTX_LEN = 256
GRID_W = 64
N_MOD = 9
EPS = 1e-6
NEG_INF = -1e30
HALF_STEP = 0.5
D_FF = 2816
FOURIER_GROUPS = 4
FOURIER_GW = D_MODEL // 8
D_FOURIER = FOURIER_GROUPS * FOURIER_GW
D_CONV = D_MODEL // 2
CONV_W = 31
EV_IN = D_FOURIER + 2 * D_CONV
EV_OUT = D_FOURIER + D_CONV
HEAD_DIM = 64
ROPE_BASE = 10000.0
H_DIFF = 4
DIFF_QK = H_DIFF * 2 * HEAD_DIM
DIFF_V = H_DIFF * 2 * HEAD_DIM
H_WIN = 8
H_WIN_KV = 2
WIN_G = H_WIN // H_WIN_KV
WIN_Q = H_WIN * HEAD_DIM
WIN_KV = H_WIN_KV * HEAD_DIM
WINDOW = 128
Q_BLOCK = 128
SPAN = Q_BLOCK + 2 * WINDOW
OD_SPLITS = (DIFF_QK, 2 * DIFF_QK, 2 * DIFF_QK + DIFF_V, 2 * DIFF_QK + DIFF_V + WIN_Q, 2 * DIFF_QK + DIFF_V + WIN_Q + WIN_KV)
OD_IN = 2 * DIFF_QK + DIFF_V + WIN_Q + 2 * WIN_KV
OD_OUT = DIFF_V + WIN_Q

kernel_name = 'hybrid_fourier_conv_diffattn_swa_flow_block'


def rms_norm(x, g):
    xf = x.astype(jnp.float32)
    y = xf * lax.rsqrt(jnp.mean(xf * xf, axis=-1, keepdims=True) + EPS)
    return (y * g.astype(jnp.float32)).astype(x.dtype)


def layer_norm(x, g, b):
    xf = x.astype(jnp.float32)
    xc = xf - jnp.mean(xf, axis=-1, keepdims=True)
    var = jnp.mean(xc * xc, axis=-1, keepdims=True)
    return (xc * lax.rsqrt(var + EPS) * g.astype(jnp.float32) + b.astype(jnp.float32)).astype(x.dtype)


def swiglu(h, w_in, w_out):
    gate, up = jnp.split(h @ w_in, 2, axis=-1)
    return (jax.nn.silu(gate) * up) @ w_out


def axial_rope_tables(rows):
    t = jnp.arange(rows * GRID_W)
    row = (t // GRID_W).astype(jnp.float32)
    col = (t % GRID_W).astype(jnp.float32)
    axis_dim = HEAD_DIM // 2
    inv_freq = ROPE_BASE ** (-jnp.arange(0, axis_dim, 2, dtype=jnp.float32) / axis_dim)
    ang_r = row[:, None] * inv_freq[None, :]
    ang_c = col[:, None] * inv_freq[None, :]
    return (jnp.cos(ang_r), jnp.sin(ang_r), jnp.cos(ang_c), jnp.sin(ang_c))


def _rotate(x, cos, sin):
    x1, x2 = jnp.split(x, 2, axis=-1)
    return jnp.concatenate([x1 * cos - x2 * sin, x2 * cos + x1 * sin], axis=-1)


def apply_axial_rope(x, rope):
    shp = (x.shape[1],) + (1,) * (x.ndim - 3) + (-1,)
    cr, sr, cc, sc = (t.reshape(shp).astype(x.dtype) for t in rope)
    xr, xc = jnp.split(x, 2, axis=-1)
    return jnp.concatenate([_rotate(xr, cr, sr), _rotate(xc, cc, sc)], axis=-1)


def fourier_conv_mix(h, w_in, conv_w, conv_b, ln_g, ln_b, w_out):
    B, L, _ = h.shape
    a, g = jnp.split(h @ w_in, [D_FOURIER], axis=-1)
    af = a.astype(jnp.float32).reshape(B, L, FOURIER_GROUPS, FOURIER_GW)
    fa = jnp.fft.fft2(af, axes=(1, 3), norm='ortho').real.astype(h.dtype).reshape(B, L, D_FOURIER)
    u = g[..., :D_CONV] * jax.nn.sigmoid(g[..., D_CONV:])
    u = lax.conv_general_dilated(u, conv_w[:, None, :], window_strides=(1,),
                                 padding=[(CONV_W // 2, CONV_W // 2)],
                                 dimension_numbers=('NWC', 'WIO', 'NWC'),
                                 feature_group_count=D_CONV) + conv_b
    u = jax.nn.silu(layer_norm(u, ln_g, ln_b))
    return jnp.concatenate([fa, u], axis=-1) @ w_out


def diff_attend(q1, q2, k1, k2, v, lam):
    scale = HEAD_DIM ** -0.5
    p1 = jax.nn.softmax(jnp.einsum('bhqd,bhkd->bhqk', q1, k1).astype(jnp.float32) * scale, axis=-1)
    p2 = jax.nn.softmax(jnp.einsum('bhqd,bhkd->bhqk', q2, k2).astype(jnp.float32) * scale, axis=-1)
    return jnp.einsum('bhqk,bhkd->bhqd', (p1 - lam * p2).astype(v.dtype), v)


def sink_attend(q, k_ctx, v_ctx, sink, k_win=None, v_win=None, mask=None):
    scale = HEAD_DIM ** -0.5
    B, Q = q.shape[:2]
    n_ctx = k_ctx.shape[1]
    parts = [jnp.broadcast_to(sink.astype(jnp.float32)[None, :, :, None, None], (B, H_WIN_KV, WIN_G, Q, 1)),
             jnp.einsum('bqhgd,bkhd->bhgqk', q, k_ctx).astype(jnp.float32) * scale]
    if k_win is not None:
        s_w = jnp.einsum('bqhgd,bkhd->bhgqk', q, k_win).astype(jnp.float32) * scale
        parts.append(jnp.where(mask, s_w, NEG_INF))
    p = jax.nn.softmax(jnp.concatenate(parts, axis=-1), axis=-1).astype(v_ctx.dtype)
    out = jnp.einsum('bhgqk,bkhd->bqhgd', p[..., 1:1 + n_ctx], v_ctx)
    if k_win is not None:
        out = out + jnp.einsum('bhgqk,bkhd->bqhgd', p[..., 1 + n_ctx:], v_win)
    return out


def diff_window_mix(hx, hc, w_in, lam_vec, subln_g, sink, w_out, lam_init, rope, ctx_out):
    B, S, _ = hx.shape
    C = hc.shape[1]
    nb = S // Q_BLOCK
    lv = lam_vec.astype(jnp.float32)
    lam = jnp.exp(jnp.sum(lv[0] * lv[1])) - jnp.exp(jnp.sum(lv[2] * lv[3])) + lam_init
    sink_g = sink.reshape(H_WIN_KV, WIN_G)

    def project(h):
        L = h.shape[1]
        dq, dk, dv, wq, wk, wv = jnp.split(h @ w_in, OD_SPLITS, axis=-1)
        return (dq.reshape(B, L, H_DIFF, 2, HEAD_DIM), dk.reshape(B, L, H_DIFF, 2, HEAD_DIM),
                dv.reshape(B, L, H_DIFF, 2 * HEAD_DIM), wq.reshape(B, L, H_WIN_KV, WIN_G, HEAD_DIM),
                wk.reshape(B, L, H_WIN_KV, HEAD_DIM), wv.reshape(B, L, H_WIN_KV, HEAD_DIM))

    dqx, dkx, dvx, wqx, wkx, wvx = project(hx)
    dqc, dkc, dvc, wqc, wkc, wvc = project(hc)
    dqx, dkx, wqx, wkx = (apply_axial_rope(t, rope) for t in (dqx, dkx, wqx, wkx))

    dk_all = jnp.concatenate([dkc, dkx], axis=1)
    k1 = dk_all[:, :, :, 0].transpose(0, 2, 1, 3)
    k2 = dk_all[:, :, :, 1].transpose(0, 2, 1, 3)
    v_all = jnp.concatenate([dvc, dvx], axis=1).transpose(0, 2, 1, 3)

    def to_blocks(q):
        return q.reshape(B, nb, Q_BLOCK, H_DIFF, HEAD_DIM).transpose(1, 0, 3, 2, 4)

    o = lax.map(lambda qs: diff_attend(qs[0], qs[1], k1, k2, v_all, lam),
                (to_blocks(dqx[:, :, :, 0]), to_blocks(dqx[:, :, :, 1])))
    o = o.transpose(1, 0, 3, 2, 4).reshape(B, S, H_DIFF, 2 * HEAD_DIM)
    diff_x = (rms_norm(o, subln_g) * (1.0 - lam_init)).reshape(B, S, DIFF_V)

    kp = jnp.pad(wkx, ((0, 0), (WINDOW, WINDOW), (0, 0), (0, 0)))
    vp = jnp.pad(wvx, ((0, 0), (WINDOW, WINDOW), (0, 0), (0, 0)))
    qb = jnp.moveaxis(wqx.reshape(B, nb, Q_BLOCK, H_WIN_KV, WIN_G, HEAD_DIM), 1, 0)
    offs_q = jnp.arange(Q_BLOCK)
    offs_k = jnp.arange(SPAN)

    def win_block(args):
        i, q = args
        start = i * Q_BLOCK
        kw = lax.dynamic_slice_in_dim(kp, start, SPAN, axis=1)
        vw = lax.dynamic_slice_in_dim(vp, start, SPAN, axis=1)
        qpos = start + offs_q
        kpos = start - WINDOW + offs_k
        mask = (jnp.abs(kpos[None, :] - qpos[:, None]) <= WINDOW) & (kpos >= 0)[None, :] & (kpos < S)[None, :]
        return sink_attend(q, wkc, wvc, sink_g, kw, vw, mask)

    wo = lax.map(win_block, (jnp.arange(nb), qb))
    win_x = jnp.moveaxis(wo, 0, 1).reshape(B, S, WIN_Q)
    yx = jnp.concatenate([diff_x, win_x], axis=-1) @ w_out
    if not ctx_out:
        return yx, None

    oc = diff_attend(dqc[:, :, :, 0].transpose(0, 2, 1, 3), dqc[:, :, :, 1].transpose(0, 2, 1, 3),
                     dkc[:, :, :, 0].transpose(0, 2, 1, 3), dkc[:, :, :, 1].transpose(0, 2, 1, 3),
                     dvc.transpose(0, 2, 1, 3), lam)
    diff_c = (rms_norm(oc.transpose(0, 2, 1, 3), subln_g) * (1.0 - lam_init)).reshape(B, C, DIFF_V)
    win_c = sink_attend(wqc, wkc, wvc, sink_g).reshape(B, C, WIN_Q)
    yc = jnp.concatenate([diff_c, win_c], axis=-1) @ w_out
    return yx, yc


def _pre(h, g, m, s):
    return rms_norm(h, g) * (1 + m[3 * s + 1]) + m[3 * s]


def _post(h, y, g, m, s, w):
    return h + w * m[3 * s + 2] * rms_norm(y, g)


def setup_inputs(seed: int = 0) -> dict:
    key = jax.random.key(seed)
    ks = jax.random.split(key, 21)
    n_even = (DEPTH + 1) // 2
    n_odd = DEPTH // 2
    D = D_MODEL

    def nrm(k, shape, scale):
        return jax.random.normal(k, shape, jnp.float32) * scale

    return {
        'x': nrm(ks[0], (BATCH, SEQ, D), 1.0),
        'c': nrm(ks[1], (BATCH, D), 1.0),
        'ctx': nrm(ks[2], (BATCH, CTX_LEN, D), 1.0),
        'c_ctx': nrm(ks[3], (D,), 1.0),
        'w_mod': nrm(ks[4], (DEPTH, D, N_MOD * D), 0.5 * D ** -0.5),
        'b_mod': nrm(ks[5], (DEPTH, N_MOD * D), 0.02),
        'norm_pre': 1.0 + nrm(ks[6], (DEPTH, 3, D), 0.05),
        'norm_post': 1.0 + nrm(ks[7], (DEPTH, 3, D), 0.05),
        'ffn_w_in': nrm(ks[8], (DEPTH, 2, D, 2 * D_FF), D ** -0.5),
        'ffn_w_out': nrm(ks[9], (DEPTH, 2, D_FF, D), D_FF ** -0.5),
        'ev_w_in': nrm(ks[10], (n_even, D, EV_IN), D ** -0.5),
        'ev_conv_w': nrm(ks[11], (n_even, CONV_W, D_CONV), CONV_W ** -0.5),
        'ev_conv_b': nrm(ks[12], (n_even, D_CONV), 0.02),
        'ev_ln_g': 1.0 + nrm(ks[13], (n_even, D_CONV), 0.05),
        'ev_ln_b': nrm(ks[14], (n_even, D_CONV), 0.02),
        'ev_w_out': nrm(ks[15], (n_even, EV_OUT, D), EV_OUT ** -0.5),
        'od_w_in': nrm(ks[16], (n_odd, D, OD_IN), D ** -0.5),
        'od_lambda': nrm(ks[17], (n_odd, 4, HEAD_DIM), 0.1),
        'od_subln_g': 1.0 + nrm(ks[18], (n_odd, 2 * HEAD_DIM), 0.05),
        'od_sink': nrm(ks[19], (n_odd, H_WIN), 0.5),
        'od_w_out': nrm(ks[20], (n_odd, OD_OUT, D), OD_OUT ** -0.5),
    }


def reference(x, c, ctx, c_ctx, w_mod, b_mod, norm_pre, norm_post, ffn_w_in, ffn_w_out,
              ev_w_in, ev_conv_w, ev_conv_b, ev_ln_g, ev_ln_b, ev_w_out,
              od_w_in, od_lambda, od_subln_g, od_sink, od_w_out):
    B, n_tok, D = x.shape
    rows = n_tok // GRID_W
    rope = axial_rope_tables(rows)
    for l in range(DEPTH):
        last = l == DEPTH - 1
        odd = l % 2 == 1
        ctx_in = (not last) or odd
        ctx_out = not last
        mx = (jax.nn.silu(c) @ w_mod[l] + b_mod[l]).reshape(B, N_MOD, D).transpose(1, 0, 2)[:, :, None, :]
        mc = (jax.nn.silu(c_ctx) @ w_mod[l] + b_mod[l]).reshape(N_MOD, 1, 1, D)
        x = _post(x, swiglu(_pre(x, norm_pre[l, 0], mx, 0), ffn_w_in[l, 0], ffn_w_out[l, 0]),
                  norm_post[l, 0], mx, 0, HALF_STEP)
        if ctx_in:
            ctx = _post(ctx, swiglu(_pre(ctx, norm_pre[l, 0], mc, 0), ffn_w_in[l, 0], ffn_w_out[l, 0]),
                        norm_post[l, 0], mc, 0, HALF_STEP)
        hx = _pre(x, norm_pre[l, 1], mx, 1)
        j = l // 2
        if odd:
            hc = _pre(ctx, norm_pre[l, 1], mc, 1)
            lam_init = 0.8 - 0.6 * math.exp(-0.3 * l)
            yx, yc = diff_window_mix(hx, hc, od_w_in[j], od_lambda[j], od_subln_g[j], od_sink[j],
                                     od_w_out[j], lam_init, rope, ctx_out)
        else:
            yx = fourier_conv_mix(hx, ev_w_in[j], ev_conv_w[j], ev_conv_b[j], ev_ln_g[j], ev_ln_b[j], ev_w_out[j])
            yc = None
            if ctx_out:
                hc = _pre(ctx, norm_pre[l, 1], mc, 1)
                yc = fourier_conv_mix(hc, ev_w_in[j], ev_conv_w[j], ev_conv_b[j], ev_ln_g[j], ev_ln_b[j], ev_w_out[j])
        x = _post(x, yx, norm_post[l, 1], mx, 1, 1.0)
        if ctx_out:
            ctx = _post(ctx, yc, norm_post[l, 1], mc, 1, 1.0)
        x = _post(x, swiglu(_pre(x, norm_pre[l, 2], mx, 2), ffn_w_in[l, 1], ffn_w_out[l, 1]),
                  norm_post[l, 2], mx, 2, HALF_STEP)
        if ctx_out:
            ctx = _post(ctx, swiglu(_pre(ctx, norm_pre[l, 2], mc, 2), ffn_w_in[l, 1], ffn_w_out[l, 1]),
                        norm_post[l, 2], mc, 2, HALF_STEP)
    return x
```

```python
import functools
import math

import numpy as np
import jax
import jax.numpy as jnp
from jax import lax
from jax.experimental import pallas as pl
from jax.experimental.pallas import tpu as pltpu

F32 = jnp.float32
BF16 = jnp.bfloat16

D_MODEL = 1024
DEPTH = 2
GRID_W = 64
N_MOD = 9
EPS = 1e-6
NEG_INF = -1e30
HALF_STEP = 0.5
D_FF = 2816
FOURIER_GROUPS = 4
FOURIER_GW = D_MODEL // 8
D_FOURIER = FOURIER_GROUPS * FOURIER_GW
D_CONV = D_MODEL // 2
CONV_W = 31
HEAD_DIM = 64
ROPE_BASE = 10000.0
H_DIFF = 4
DIFF_QK = H_DIFF * 2 * HEAD_DIM
DIFF_V = H_DIFF * 2 * HEAD_DIM
H_WIN = 8
H_WIN_KV = 2
WIN_G = H_WIN // H_WIN_KV
WIN_Q = H_WIN * HEAD_DIM
WIN_KV = H_WIN_KV * HEAD_DIM
WINDOW = 128
Q_BLOCK = 128

V7X_LANES = 128
V7X_MXU_DIM = 256
V7X_BF16_SUBLANES = 16
V7X_VMEM_LIMIT_BYTES = 56 * 1024 * 1024

FFN_TM = 512
FFN_FC = V7X_MXU_DIM
ROW_TM = 256
KV_CHUNK = 256
FFT_RB = 8
CONV_RB = 32
CONV_HALO = V7X_BF16_SUBLANES


def _cparams(*sem):
    return pltpu.CompilerParams(dimension_semantics=sem, vmem_limit_bytes=V7X_VMEM_LIMIT_BYTES)


def _const_spec(shape):
    n = len(shape)
    return pl.BlockSpec(shape, lambda *_: (0,) * n, pipeline_mode=pl.Buffered(1))


def _dot(a, b):
    return jnp.dot(a, b, preferred_element_type=F32)


def _dot_nt(a, b):
    return lax.dot_general(a, b, (((1,), (1,)), ((), ())), preferred_element_type=F32)


def _rms(xf, g):
    return xf * lax.rsqrt(jnp.mean(xf * xf, axis=-1, keepdims=True) + EPS) * g


def _mrow(m_ref, j):
    return m_ref[0, j:j + 1, :]


def _pre(x, g, m_ref, s):
    return _rms(x, g) * (1.0 + _mrow(m_ref, 3 * s + 1)) + _mrow(m_ref, 3 * s)


def _mod_spec(mod, tiles_per_batch):
    d = mod.shape[-1]
    if mod.shape[0] == 1:
        return pl.BlockSpec((1, N_MOD, d), lambda i: (0, 0, 0))
    return pl.BlockSpec((1, N_MOD, d), lambda i: (i // tiles_per_batch, 0, 0))


def _sigmoid(x):
    return 1.0 / (1.0 + jnp.exp(-x))


def _mod_kernel(c_ref, w_ref, b_ref, o_ref):
    c = c_ref[...]
    h = c * _sigmoid(c)
    o_ref[0] = jnp.dot(h, w_ref[0], preferred_element_type=F32,
                       precision=lax.Precision.HIGHEST) + b_ref[0]


def _modulation(cs, w_mod, b_mod):
    depth, d, nd = w_mod.shape
    rows = cs.shape[0]
    return pl.pallas_call(
        _mod_kernel,
        out_shape=jax.ShapeDtypeStruct((depth, rows, nd), F32),
        grid=(depth, nd // d),
        in_specs=[pl.BlockSpec((rows, d), lambda l, j: (0, 0)),
                  pl.BlockSpec((1, d, d), lambda l, j: (l, 0, j)),
                  pl.BlockSpec((1, 1, d), lambda l, j: (l, 0, j))],
        out_specs=pl.BlockSpec((1, rows, d), lambda l, j: (l, 0, j)),
        compiler_params=_cparams("arbitrary", "arbitrary"),
        name="modulation",
    )(cs, w_mod, b_mod.reshape(depth, 1, nd))


def _ffn_kernel(s, x_ref, m_ref, gpre_ref, gpost_ref, win_ref, wout_ref, o_ref, a_ref):
    x = x_ref[...]
    h = _pre(x, gpre_ref[...], m_ref, s).astype(BF16)
    for c in range(D_FF // FFN_FC):
        lo = c * FFN_FC
        g = _dot(h, win_ref[:, lo:lo + FFN_FC])
        u = _dot(h, win_ref[:, D_FF + lo:D_FF + lo + FFN_FC])
        a_ref[:, lo:lo + FFN_FC] = (g * _sigmoid(g) * u).astype(BF16)
    y = _dot(a_ref[...], wout_ref[...])
    o_ref[...] = x + HALF_STEP * _mrow(m_ref, 3 * s + 2) * _rms(y, gpost_ref[...])


def _ffn(x2d, mod, gpre, gpost, w_in, w_out, s, tm, tiles_per_batch):
    t, d = x2d.shape
    return pl.pallas_call(
        functools.partial(_ffn_kernel, s),
        out_shape=jax.ShapeDtypeStruct((t, d), F32),
        grid=(t // tm,),
        in_specs=[pl.BlockSpec((tm, d), lambda i: (i, 0)),
                  _mod_spec(mod, tiles_per_batch),
                  _const_spec((1, d)), _const_spec((1, d)),
                  _const_spec(w_in.shape), _const_spec(w_out.shape)],
        out_specs=pl.BlockSpec((tm, d), lambda i: (i, 0)),
        scratch_shapes=[pltpu.VMEM((tm, D_FF), BF16)],
        compiler_params=_cparams("arbitrary"),
        name="swiglu_halfstep",
    )(x2d, mod, gpre, gpost, w_in, w_out)


def _outproj_kernel(a_ref, b_ref, x_ref, m_ref, gpost_ref, w_ref, o_ref):
    ka = a_ref.shape[1]
    y = _dot(a_ref[...].astype(BF16), w_ref[:ka]) + _dot(b_ref[...].astype(BF16), w_ref[ka:])
    o_ref[...] = x_ref[...] + _mrow(m_ref, 5) * _rms(y, gpost_ref[...])


def _outproj(a, b, x2d, mod, gpost, w_out, tm, tiles_per_batch):
    t, d = x2d.shape
    return pl.pallas_call(
        _outproj_kernel,
        out_shape=jax.ShapeDtypeStruct((t, d), F32),
        grid=(t // tm,),
        in_specs=[pl.BlockSpec((tm, a.shape[1]), lambda i: (i, 0)),
                  pl.BlockSpec((tm, b.shape[1]), lambda i: (i, 0)),
                  pl.BlockSpec((tm, d), lambda i: (i, 0)),
                  _mod_spec(mod, tiles_per_batch),
                  _const_spec((1, d)), _const_spec(w_out.shape)],
        out_specs=pl.BlockSpec((tm, d), lambda i: (i, 0)),
        compiler_params=_cparams("arbitrary"),
        name="mixer_outproj",
    )(a, b, x2d, mod, gpost, w_out)


def _ev_proj_kernel(x_ref, m_ref, gpre_ref, w_ref, cw_ref, p_ref, u_ref):
    h = _pre(x_ref[...], gpre_ref[...], m_ref, 1).astype(BF16)
    a = _dot(h, w_ref[:, :D_FOURIER])
    g1 = _dot(h, w_ref[:, D_FOURIER:D_FOURIER + D_CONV])
    g2 = _dot(h, w_ref[:, D_FOURIER + D_CONV:])
    u_ref[...] = (g1 * _sigmoid(g2)).astype(BF16)
    gw = FOURIER_GW
    for g in range(FOURIER_GROUPS):
        pg = _dot(a[:, g * gw:(g + 1) * gw].astype(BF16), cw_ref[...])
        p_ref[0, 0, :, g * gw:(g + 1) * gw] = pg[:, :gw].astype(BF16)
        p_ref[0, 1, :, g * gw:(g + 1) * gw] = pg[:, gw:].astype(BF16)


def _ev_proj(x2d, mod, gpre, w_in, cw, nbatch, tm):
    t, d = x2d.shape
    seq = t // nbatch
    tpb = seq // tm
    return pl.pallas_call(
        _ev_proj_kernel,
        out_shape=(jax.ShapeDtypeStruct((nbatch, 2, seq, D_FOURIER), BF16),
                   jax.ShapeDtypeStruct((t, D_CONV), BF16)),
        grid=(t // tm,),
        in_specs=[pl.BlockSpec((tm, d), lambda i: (i, 0)),
                  _mod_spec(mod, tpb),
                  _const_spec((1, d)), _const_spec(w_in.shape), _const_spec(cw.shape)],
        out_specs=(pl.BlockSpec((1, 2, tm, D_FOURIER), lambda i: (i // tpb, 0, i % tpb, 0)),
                   pl.BlockSpec((tm, D_CONV), lambda i: (i, 0))),
        compiler_params=_cparams("arbitrary"),
        name="even_proj",
    )(x2d, mod, gpre, w_in, cw)


def _ev_fft_a_kernel(x_ref, m_ref, gpre_ref, w_ref, cw_ref, ka_ref, twc_ref, tws_ref, z_ref, u_ref):
    n1c = x_ref.shape[1]
    rows = n1c * FFT_RB
    x = x_ref[0].reshape(rows, D_MODEL)
    h = _pre(x, gpre_ref[...], m_ref, 1).astype(BF16)
    a = _dot(h, w_ref[:, :D_FOURIER])
    g1 = _dot(h, w_ref[:, D_FOURIER:D_FOURIER + D_CONV])
    g2 = _dot(h, w_ref[:, D_FOURIER + D_CONV:])
    u_ref[0] = (g1 * _sigmoid(g2)).reshape(n1c, FFT_RB, D_CONV)
    gw = FOURIER_GW
    pr, pi = [], []
    for g in range(FOURIER_GROUPS):
        pg = _dot(a[:, g * gw:(g + 1) * gw].astype(BF16), cw_ref[...])
        pr.append(pg[:, :gw])
        pi.append(pg[:, gw:])
    p = jnp.concatenate([jnp.concatenate(pr, axis=1), jnp.concatenate(pi, axis=1)], axis=0)
    z = _dot(ka_ref[...], p.astype(BF16))
    zr, zi = z[:rows], z[rows:]
    tc = jnp.concatenate([twc_ref[...].reshape(rows, V7X_LANES)] * FOURIER_GROUPS, axis=1)
    ts = jnp.concatenate([tws_ref[...].reshape(rows, V7X_LANES)] * FOURIER_GROUPS, axis=1)
    z_ref[0, 0] = (zr * tc + zi * ts).reshape(n1c, FFT_RB, D_FOURIER)
    z_ref[0, 1] = (zi * tc - zr * ts).reshape(n1c, FFT_RB, D_FOURIER)


def _ev_fft_a(x2d, mod, gpre, w_in, cw, ka, twc, tws, nbatch):
    t, d = x2d.shape
    seq = t // nbatch
    n1c = seq // V7X_LANES
    nrb = V7X_LANES // FFT_RB
    x4 = x2d.reshape(nbatch, n1c, V7X_LANES, d)
    blk = lambda w: pl.BlockSpec((1, n1c, FFT_RB, w), lambda i: (i // nrb, 0, i % nrb, 0))
    return pl.pallas_call(
        _ev_fft_a_kernel,
        out_shape=(jax.ShapeDtypeStruct((nbatch, 2, n1c, V7X_LANES, D_FOURIER), F32),
                   jax.ShapeDtypeStruct((nbatch, n1c, V7X_LANES, D_CONV), F32)),
        grid=(nbatch * nrb,),
        in_specs=[blk(d), _mod_spec(mod, nrb), _const_spec((1, d)), _const_spec(w_in.shape),
                  _const_spec(cw.shape), _const_spec(ka.shape),
                  pl.BlockSpec((n1c, FFT_RB, V7X_LANES), lambda i: (0, i % nrb, 0)),
                  pl.BlockSpec((n1c, FFT_RB, V7X_LANES), lambda i: (0, i % nrb, 0))],
        out_specs=(pl.BlockSpec((1, 2, n1c, FFT_RB, D_FOURIER), lambda i: (i // nrb, 0, 0, i % nrb, 0)),
                   blk(D_CONV)),
        compiler_params=_cparams("arbitrary"),
        name="even_proj_fft_a",
    )(x4, mod, gpre, w_in, cw, ka, twc, tws)


def _ev_fft_b_kernel(kb_ref, z_ref, o_ref):
    z = z_ref[0].reshape(2 * FFT_RB * V7X_LANES, D_FOURIER).astype(BF16)
    o_ref[0] = _dot(kb_ref[...], z).reshape(V7X_LANES, FFT_RB, D_FOURIER)


def _ev_fft_b(z, kb):
    nbatch, _, n1c, _, nc = z.shape
    return pl.pallas_call(
        _ev_fft_b_kernel,
        out_shape=jax.ShapeDtypeStruct((nbatch, V7X_LANES, n1c, nc), F32),
        grid=(nbatch, n1c // FFT_RB),
        in_specs=[_const_spec(kb.shape),
                  pl.BlockSpec((1, 2, FFT_RB, V7X_LANES, nc), lambda b, j: (b, 0, j, 0, 0))],
        out_specs=pl.BlockSpec((1, V7X_LANES, FFT_RB, nc), lambda b, j: (b, 0, j, 0)),
        compiler_params=_cparams("arbitrary", "arbitrary"),
        name="fft_b",
    )(kb, z)


def _dft_real_kernel(mat_ref, z_ref, o_ref):
    o_ref[0] = _dot(mat_ref[...], z_ref[0]).astype(o_ref.dtype)


def _dft_real(z, mat, nb):
    nbatch, k, n = z.shape
    m = mat.shape[0]
    return pl.pallas_call(
        _dft_real_kernel,
        out_shape=jax.ShapeDtypeStruct((nbatch, m, n), BF16),
        grid=(nbatch, n // nb),
        in_specs=[_const_spec(mat.shape), pl.BlockSpec((1, k, nb), lambda b, j: (b, 0, j))],
        out_specs=pl.BlockSpec((1, m, nb), lambda b, j: (b, 0, j)),
        compiler_params=_cparams("arbitrary", "arbitrary"),
        name="dft_real",
    )(mat, z)


def _conv_kernel(tpb, u_ref, up_ref, un_ref, cw_ref, cb_ref, lg_ref, lb_ref, o_ref, ext_ref):
    tm = u_ref.shape[0]
    j = pl.program_id(0) % tpb
    halo = CONV_HALO
    ext_ref[0:halo] = jnp.where(j > 0, up_ref[...].astype(F32), 0.0)
    ext_ref[halo:halo + tm] = u_ref[...].astype(F32)
    ext_ref[halo + tm:halo + tm + halo] = jnp.where(j < tpb - 1, un_ref[...].astype(F32), 0.0)
    cb, lg, lb = cb_ref[...], lg_ref[...], lb_ref[...]
    first = halo - CONV_W // 2

    def block(rb, carry):
        base = pl.multiple_of(rb * CONV_RB, CONV_RB)
        blk = ext_ref[pl.ds(base, CONV_RB + 2 * halo), :]
        acc = jnp.zeros((CONV_RB, D_CONV), F32)
        for t in range(CONV_W):
            acc = acc + cw_ref[t:t + 1, :] * blk[first + t:first + t + CONV_RB, :]
        v = acc + cb
        vc = v - jnp.mean(v, axis=-1, keepdims=True)
        var = jnp.mean(vc * vc, axis=-1, keepdims=True)
        y = vc * lax.rsqrt(var + EPS) * lg + lb
        o_ref[pl.ds(base, CONV_RB), :] = (y * _sigmoid(y)).astype(BF16)
        return carry

    lax.fori_loop(0, tm // CONV_RB, block, 0)


def _conv_ln_swish(u, conv_w, conv_b, ln_g, ln_b, nbatch, tm):
    t, dc = u.shape
    tpb = (t // nbatch) // tm
    hb = tm // CONV_HALO
    last = t // CONV_HALO - 1
    return pl.pallas_call(
        functools.partial(_conv_kernel, tpb),
        out_shape=jax.ShapeDtypeStruct((t, dc), BF16),
        grid=(t // tm,),
        in_specs=[pl.BlockSpec((tm, dc), lambda i: (i, 0)),
                  pl.BlockSpec((CONV_HALO, dc), lambda i: (jnp.maximum(i * hb - 1, 0), 0)),
                  pl.BlockSpec((CONV_HALO, dc), lambda i: (jnp.minimum((i + 1) * hb, last), 0)),
                  _const_spec(conv_w.shape), _const_spec((1, dc)), _const_spec((1, dc)),
                  _const_spec((1, dc))],
        out_specs=pl.BlockSpec((tm, dc), lambda i: (i, 0)),
        scratch_shapes=[pltpu.VMEM((tm + 2 * CONV_HALO, dc), F32)],
        compiler_params=_cparams("arbitrary"),
        name="conv_ln_swish",
    )(u, u, u, conv_w, conv_b, ln_g, ln_b)


def _rope(z, cos, sin, even_block):
    out = []
    for j in range(z.shape[1] // V7X_LANES):
        zs = z[:, j * V7X_LANES:(j + 1) * V7X_LANES]
        sw = jnp.where(even_block, pltpu.roll(zs, V7X_LANES - 16, 1), pltpu.roll(zs, 16, 1))
        out.append(zs * cos + sw * sin)
    return out[0] if len(out) == 1 else jnp.concatenate(out, axis=1)


def _od_proj_kernel(x_ref, m_ref, gpre_ref, w_ref, cos_ref, sin_ref,
                    dq_ref, dk_ref, vt_ref, wq_ref, wk_ref, wv_ref):
    h = _pre(x_ref[...], gpre_ref[...], m_ref, 1).astype(BF16)
    cos, sin = cos_ref[...], sin_ref[...]
    lane = lax.broadcasted_iota(jnp.int32, (1, V7X_LANES), 1)
    even_block = ((lane >> 4) & 1) == 0
    scale = HEAD_DIM ** -0.5
    o = 0
    dq_ref[...] = (_rope(_dot(h, w_ref[:, o:o + DIFF_QK]), cos, sin, even_block) * scale).astype(BF16)
    o += DIFF_QK
    dk_ref[...] = _rope(_dot(h, w_ref[:, o:o + DIFF_QK]), cos, sin, even_block).astype(BF16)
    o += DIFF_QK
    vt_ref[0, 0] = _dot(h, w_ref[:, o:o + DIFF_V]).T.astype(BF16)
    o += DIFF_V
    wq_ref[...] = (_rope(_dot(h, w_ref[:, o:o + WIN_Q]), cos, sin, even_block) * scale).astype(BF16)
    o += WIN_Q
    wk_ref[...] = _rope(_dot(h, w_ref[:, o:o + WIN_KV]), cos, sin, even_block).astype(BF16)
    o += WIN_KV
    wv_ref[...] = _dot(h, w_ref[:, o:o + WIN_KV]).astype(BF16)


def _od_proj(x2d, mod, gpre, w_in, cos, sin, nbatch, tm):
    t, d = x2d.shape
    seq = t // nbatch
    tpb = seq // tm
    row = lambda i: (i, 0)
    return pl.pallas_call(
        _od_proj_kernel,
        out_shape=(jax.ShapeDtypeStruct((t, DIFF_QK), BF16),
                   jax.ShapeDtypeStruct((t, DIFF_QK), BF16),
                   jax.ShapeDtypeStruct((nbatch, tpb, DIFF_V, tm), BF16),
                   jax.ShapeDtypeStruct((t, WIN_Q), BF16),
                   jax.ShapeDtypeStruct((t, WIN_KV), BF16),
                   jax.ShapeDtypeStruct((t, WIN_KV), BF16)),
        grid=(t // tm,),
        in_specs=[pl.BlockSpec((tm, d), row),
                  _mod_spec(mod, tpb),
                  _const_spec((1, d)), _const_spec(w_in.shape),
                  pl.BlockSpec((tm, V7X_LANES), lambda i: (i % tpb, 0)),
                  pl.BlockSpec((tm, V7X_LANES), lambda i: (i % tpb, 0))],
        out_specs=(pl.BlockSpec((tm, DIFF_QK), row),
                   pl.BlockSpec((tm, DIFF_QK), row),
                   pl.BlockSpec((1, 1, DIFF_V, tm), lambda i: (i // tpb, i % tpb, 0, 0)),
                   pl.BlockSpec((tm, WIN_Q), row),
                   pl.BlockSpec((tm, WIN_KV), row),
                   pl.BlockSpec((tm, WIN_KV), row)),
        compiler_params=_cparams("arbitrary"),
        name="odd_proj",
    )(x2d, mod, gpre, w_in, cos, sin)


def _od_proj_ctx_kernel(x_ref, m_ref, gpre_ref, w_ref, dk_ref, vt_ref, wk_ref, wv_ref):
    h = _pre(x_ref[...], gpre_ref[...], m_ref, 1).astype(BF16)
    o = DIFF_QK
    dk_ref[...] = _dot(h, w_ref[:, o:o + DIFF_QK]).astype(BF16)
    o += DIFF_QK
    vt_ref[0] = _dot(h, w_ref[:, o:o + DIFF_V]).T.astype(BF16)
    o += DIFF_V + WIN_Q
    wk_ref[...] = _dot(h, w_ref[:, o:o + WIN_KV]).astype(BF16)
    o += WIN_KV
    wv_ref[...] = _dot(h, w_ref[:, o:o + WIN_KV]).astype(BF16)


def _od_proj_ctx(c2d, mod, gpre, w_in, nbatch):
    t, d = c2d.shape
    tm = t // nbatch
    row = lambda i: (i, 0)
    return pl.pallas_call(
        _od_proj_ctx_kernel,
        out_shape=(jax.ShapeDtypeStruct((t, DIFF_QK), BF16),
                   jax.ShapeDtypeStruct((nbatch, DIFF_V, tm), BF16),
                   jax.ShapeDtypeStruct((t, WIN_KV), BF16),
                   jax.ShapeDtypeStruct((t, WIN_KV), BF16)),
        grid=(nbatch,),
        in_specs=[pl.BlockSpec((tm, d), row), _const_spec(mod.shape),
                  _const_spec((1, d)), _const_spec(w_in.shape)],
        out_specs=(pl.BlockSpec((tm, DIFF_QK), row),
                   pl.BlockSpec((1, DIFF_V, tm), lambda i: (i, 0, 0)),
                   pl.BlockSpec((tm, WIN_KV), row),
                   pl.BlockSpec((tm, WIN_KV), row)),
        compiler_params=_cparams("arbitrary"),
        name="odd_proj_ctx",
    )(c2d, mod, gpre, w_in)


def _diff_attn_kernel(lam_init, q_ref, kx_ref, kc_ref, vx_ref, vc_ref, lam_ref, g_ref, o_ref,
                      k_all, v_all, sa_ref, sb_ref):
    tq = Q_BLOCK
    seq = q_ref.shape[1]
    ctx_len = kc_ref.shape[1]
    nchunk = vx_ref.shape[1] + 1
    nq = seq // tq
    k_all[0:ctx_len] = kc_ref[0]
    k_all[ctx_len:ctx_len + seq] = kx_ref[0]
    v_all[0] = vc_ref[0]
    v_all[1:nchunk] = vx_ref[0]
    lv = lam_ref[...]
    lam = (jnp.exp(jnp.sum(lv[0:1] * lv[1:2], axis=-1, keepdims=True))
           - jnp.exp(jnp.sum(lv[2:3] * lv[3:4], axis=-1, keepdims=True)) + lam_init)
    lane = lax.broadcasted_iota(jnp.int32, (1, 2 * HEAD_DIM), 1)
    first_half = lane < HEAD_DIM
    zero = jnp.zeros((), BF16)

    def scores(t, s_ref):
        r0 = pl.multiple_of(t * tq, tq)
        q = q_ref[0, pl.ds(r0, tq), :]
        qq = jnp.concatenate([jnp.where(first_half, q, zero),
                              jnp.where(first_half, zero, q)], axis=0)
        mcol = jnp.full((1, 2 * tq), NEG_INF, F32)
        for c in range(nchunk):
            s = _dot_nt(k_all[c * KV_CHUNK:(c + 1) * KV_CHUNK, :], qq)
            s_ref[c] = s
            mcol = jnp.maximum(mcol, jnp.max(s, axis=0, keepdims=True))
        return mcol

    def attend(t, s_ref, mcol):
        acc = jnp.zeros((2 * HEAD_DIM, 2 * tq), F32)
        l = jnp.zeros((1, 2 * tq), F32)
        for c in range(nchunk):
            p = jnp.exp(s_ref[c] - mcol)
            l = l + jnp.sum(p, axis=0, keepdims=True)
            acc = acc + _dot(v_all[c], p.astype(BF16))
        o = acc / l
        d = (o[:, :tq] - lam * o[:, tq:]).T
        y = _rms(d, g_ref[...]) * (1.0 - lam_init)
        o_ref[0, pl.ds(pl.multiple_of(t * tq, tq), tq), :] = y.astype(BF16)

    def pair(i, m_a):
        t = 2 * i
        m_b = scores(t + 1, sb_ref)
        attend(t, sa_ref, m_a)
        m_a = scores(jnp.minimum(t + 2, nq - 1), sa_ref)
        attend(t + 1, sb_ref, m_b)
        return m_a

    lax.fori_loop(0, nq // 2, pair, scores(0, sa_ref))


def _diff_attn(dq, dk, vt, dkc, vtc, lam_vec, subln_g, lam_init):
    nbatch, seq, _ = dq.shape
    ctx_len = dkc.shape[1]
    nchunk = vt.shape[1]
    hd2 = 2 * HEAD_DIM
    return pl.pallas_call(
        functools.partial(_diff_attn_kernel, lam_init),
        out_shape=jax.ShapeDtypeStruct((nbatch, seq, DIFF_V), BF16),
        grid=(nbatch, H_DIFF),
        in_specs=[pl.BlockSpec((1, seq, hd2), lambda b, h: (b, 0, h)),
                  pl.BlockSpec((1, seq, hd2), lambda b, h: (b, 0, h)),
                  pl.BlockSpec((1, ctx_len, hd2), lambda b, h: (b, 0, h)),
                  pl.BlockSpec((1, nchunk, hd2, KV_CHUNK), lambda b, h: (b, 0, h, 0)),
                  pl.BlockSpec((1, hd2, ctx_len), lambda b, h: (b, h, 0)),
                  _const_spec(lam_vec.shape), _const_spec((1, hd2))],
        out_specs=pl.BlockSpec((1, seq, hd2), lambda b, h: (b, 0, h)),
        scratch_shapes=[pltpu.VMEM((ctx_len + seq, hd2), BF16),
                        pltpu.VMEM((nchunk + 1, hd2, KV_CHUNK), BF16),
                        pltpu.VMEM((nchunk + 1, KV_CHUNK, 2 * Q_BLOCK), F32),
                        pltpu.VMEM((nchunk + 1, KV_CHUNK, 2 * Q_BLOCK), F32)],
        compiler_params=_cparams("arbitrary", "arbitrary"),
        name="diff_attention",
    )(dq, dk, dkc, vt, vtc, lam_vec, subln_g)


WIN_GW = WIN_G * HEAD_DIM
WIN_REP_ROWS = 512


def _win_attn_kernel(q_ref, k_ref, v_ref, kc_ref, vc_ref, rep_ref, sink_ref, o_ref, kr_ref, vr_ref):
    seq = q_ref.shape[1]
    tq = Q_BLOCK
    span = tq + 2 * WINDOW
    kvh = pl.program_id(1)
    rep = rep_ref[0]

    def widen(c, carry):
        r0 = pl.multiple_of(c * WIN_REP_ROWS, WIN_REP_ROWS)
        kr_ref[pl.ds(r0, WIN_REP_ROWS), :] = _dot(k_ref[0, pl.ds(r0, WIN_REP_ROWS), :], rep).astype(BF16)
        vr_ref[pl.ds(r0, WIN_REP_ROWS), :] = _dot(v_ref[0, pl.ds(r0, WIN_REP_ROWS), :], rep).astype(BF16)
        return carry

    lax.fori_loop(0, seq // WIN_REP_ROWS, widen, 0)
    kcr = _dot(kc_ref[0], rep).astype(BF16)
    vcr = _dot(vc_ref[0], rep).astype(BF16)
    lane_group = lax.broadcasted_iota(jnp.int32, (1, WIN_GW), 1) >> 6
    zero = jnp.zeros((), BF16)
    qi = lax.broadcasted_iota(jnp.int32, (WIN_G * tq, span), 0) & (tq - 1)
    ki = lax.broadcasted_iota(jnp.int32, (WIN_G * tq, span), 1)
    grow = lax.broadcasted_iota(jnp.int32, (WIN_G * tq, 1), 0) >> 7
    sink = jnp.zeros((WIN_G * tq, 1), F32)
    for g in range(WIN_G):
        sink = jnp.where(grow == g, sink_ref[kvh * WIN_G + g], sink)

    def q_block(i, carry):
        r0 = pl.multiple_of(i * tq, tq)
        w0 = pl.multiple_of(jnp.clip(r0 - WINDOW, 0, seq - span), tq)
        visible = jnp.abs((w0 + ki) - (r0 + qi)) <= WINDOW
        q = q_ref[0, pl.ds(r0, tq), :]
        q4 = jnp.concatenate([jnp.where(lane_group == g, q, zero) for g in range(WIN_G)], axis=0)
        s_c = _dot_nt(q4, kcr)
        s_w = _dot_nt(q4, kr_ref[pl.ds(w0, span), :])
        s_w = jnp.where(visible, s_w, NEG_INF)
        m = jnp.maximum(jnp.maximum(jnp.max(s_c, axis=-1, keepdims=True),
                                    jnp.max(s_w, axis=-1, keepdims=True)), sink)
        e_c = jnp.exp(s_c - m)
        e_w = jnp.exp(s_w - m)
        denom = (jnp.sum(e_c, axis=-1, keepdims=True) + jnp.sum(e_w, axis=-1, keepdims=True)
                 + jnp.exp(sink - m))
        o4 = (_dot(e_c.astype(BF16), vcr) + _dot(e_w.astype(BF16), vr_ref[pl.ds(w0, span), :])) / denom
        out = jnp.zeros((tq, WIN_GW), F32)
        for g in range(WIN_G):
            out = out + jnp.where(lane_group == g, o4[g * tq:(g + 1) * tq], 0.0)
        o_ref[0, pl.ds(r0, tq), :] = out.astype(BF16)
        return carry

    lax.fori_loop(0, seq // tq, q_block, 0)


def _win_attn(wq, wk, wv, wkc, wvc, rep, sink):
    nbatch, seq, _ = wq.shape
    ctx_len = wkc.shape[1]
    per_b = lambda b, j: (b, 0, 0)
    per_bh = lambda b, j: (b, 0, j)
    return pl.pallas_call(
        _win_attn_kernel,
        out_shape=jax.ShapeDtypeStruct((nbatch, seq, WIN_Q), BF16),
        grid=(nbatch, H_WIN_KV),
        in_specs=[pl.BlockSpec((1, seq, WIN_GW), per_bh),
                  pl.BlockSpec((1, seq, WIN_KV), per_b),
                  pl.BlockSpec((1, seq, WIN_KV), per_b),
                  pl.BlockSpec((1, ctx_len, WIN_KV), per_b),
                  pl.BlockSpec((1, ctx_len, WIN_KV), per_b),
                  pl.BlockSpec((1, WIN_KV, WIN_GW), lambda b, j: (j, 0, 0)),
                  pl.BlockSpec(memory_space=pltpu.SMEM)],
        out_specs=pl.BlockSpec((1, seq, WIN_GW), per_bh),
        scratch_shapes=[pltpu.VMEM((seq, WIN_GW), BF16), pltpu.VMEM((seq, WIN_GW), BF16)],
        compiler_params=_cparams("arbitrary", "arbitrary"),
        name="window_attention",
    )(wq, wk, wv, wkc, wvc, rep, sink)


def _cos_sin(num, den):
    ang = 2.0 * np.pi * (np.asarray(num, np.int64) % den).astype(np.float64) / den
    return np.cos(ang), np.sin(ang)


def _channel_dft_table():
    k = np.arange(FOURIER_GW)
    c, s = _cos_sin(np.outer(k, k), FOURIER_GW)
    return np.concatenate([c, -s], axis=1)


def _stage_a_table(n1c):
    k = np.arange(n1c)
    c, s = _cos_sin(np.outer(k, k), n1c)
    return np.kron(np.block([[c, s], [-s, c]]), np.eye(FFT_RB))


def _stage_b_table(norm):
    k = np.arange(V7X_LANES)
    c, s = _cos_sin(np.outer(k, k), V7X_LANES)
    kb = np.zeros((V7X_LANES, FFT_RB, 2, FFT_RB, V7X_LANES))
    for i in range(FFT_RB):
        kb[:, i, 0, i, :] = c / norm
        kb[:, i, 1, i, :] = s / norm
    return kb.reshape(V7X_LANES * FFT_RB, 2 * FFT_RB * V7X_LANES)


def _twiddle_tables(seq):
    n1c = seq // V7X_LANES
    c, s = _cos_sin(np.outer(np.arange(n1c), np.arange(V7X_LANES)), seq)
    expand = lambda t: np.repeat(t[:, :, None], V7X_LANES, axis=2)
    return expand(c), expand(s)


def _real_dft_table(n, norm):
    k = np.arange(n)
    c, s = _cos_sin(np.outer(k, k), n)
    return np.concatenate([c, s], axis=1) / norm


def _rope_tables(seq):
    t = np.arange(seq)
    axis_dim = HEAD_DIM // 2
    inv_freq = ROPE_BASE ** (-np.arange(0, axis_dim, 2, dtype=np.float64) / axis_dim)
    ang_r = (t // GRID_W).astype(np.float64)[:, None] * inv_freq[None, :]
    ang_c = (t % GRID_W).astype(np.float64)[:, None] * inv_freq[None, :]
    cos = np.concatenate([np.cos(ang_r)] * 2 + [np.cos(ang_c)] * 2, axis=1)
    sin = np.concatenate([-np.sin(ang_r), np.sin(ang_r), -np.sin(ang_c), np.sin(ang_c)], axis=1)
    reps = V7X_LANES // HEAD_DIM
    return np.tile(cos, (1, reps)), np.tile(sin, (1, reps))


def _replication_table():
    rep = np.zeros((H_WIN_KV, WIN_KV, WIN_GW), np.float32)
    for j in range(H_WIN_KV):
        for g in range(WIN_G):
            for d in range(HEAD_DIM):
                rep[j, j * HEAD_DIM + d, g * HEAD_DIM + d] = 1.0
    return rep


def _fourier_conv_mix(h2d, mod, gpre, gpost, w_in, conv_w, conv_b, ln_g, ln_b, w_out, nbatch, tm):
    t = h2d.shape[0]
    seq = t // nbatch
    tpb = seq // tm
    table = lambda a: jnp.asarray(a, F32).astype(BF16)
    cw = table(_channel_dft_table())
    norm = math.sqrt(seq * FOURIER_GW)
    if seq % (FFT_RB * V7X_LANES) == 0:
        n1c = seq // V7X_LANES
        twc, tws = _twiddle_tables(seq)
        z, u = _ev_fft_a(h2d, mod, gpre, w_in, cw, table(_stage_a_table(n1c)),
                         jnp.asarray(twc, F32), jnp.asarray(tws, F32), nbatch)
        fa = _ev_fft_b(z, table(_stage_b_table(norm)))
        u = u.reshape(t, D_CONV)
    else:
        p, u = _ev_proj(h2d, mod, gpre, w_in, cw, nbatch, tm)
        fa = _dft_real(p.reshape(nbatch, 2 * seq, D_FOURIER), table(_real_dft_table(seq, norm)),
                       D_FOURIER)
    fa = fa.reshape(t, D_FOURIER)
    uc = _conv_ln_swish(u, conv_w, conv_b, ln_g, ln_b, nbatch, tm)
    return _outproj(fa, uc, h2d, mod, gpost, w_out, tm, tpb)


def _diff_window_mix(x2d, c2d, modx, modc, gpre, gpost, w_in, lam_vec, subln_g, sink, w_out,
                     lam_init, nbatch, tm):
    t = x2d.shape[0]
    seq = t // nbatch
    ctx_len = c2d.shape[0] // nbatch
    cos, sin = _rope_tables(seq)
    dq, dk, vt, wq, wk, wv = _od_proj(x2d, modx, gpre, w_in, jnp.asarray(cos, F32),
                                      jnp.asarray(sin, F32), nbatch, KV_CHUNK)
    dkc, vtc, wkc, wvc = _od_proj_ctx(c2d, modc, gpre, w_in, nbatch)
    b3 = lambda a, n: a.reshape(nbatch, n, a.shape[-1])
    diff = _diff_attn(b3(dq, seq), b3(dk, seq), vt, b3(dkc, ctx_len), vtc, lam_vec, subln_g, lam_init)
    win = _win_attn(b3(wq, seq), b3(wk, seq), b3(wv, seq), b3(wkc, ctx_len), b3(wvc, ctx_len),
                    jnp.asarray(_replication_table(), F32).astype(BF16), sink)
    return _outproj(diff.reshape(t, DIFF_V), win.reshape(t, WIN_Q), x2d, modx, gpost, w_out,
                    tm, seq // tm)


def kernel(x, c, ctx, c_ctx, w_mod, b_mod, norm_pre, norm_post, ffn_w_in, ffn_w_out, ev_w_in, ev_conv_w, ev_conv_b, ev_ln_g, ev_ln_b, ev_w_out, od_w_in, od_lambda, od_subln_g, od_sink, od_w_out):
    nbatch, seq, d = x.shape
    ctx_len = ctx.shape[1]
    depth = w_mod.shape[0]
    assert d == D_MODEL and depth == DEPTH == 2 and ctx_len == KV_CHUNK
    assert seq % FFN_TM == 0 and seq % (8 * V7X_LANES) == 0

    mod_rows = 8
    cs = jnp.concatenate([c, c_ctx[None, :], jnp.zeros((mod_rows - nbatch - 1, d), F32)], axis=0)
    mod = _modulation(cs, w_mod, b_mod).reshape(depth, mod_rows, N_MOD, d)
    row = lambda v: v.reshape(1, -1)

    x2d = x.reshape(nbatch * seq, d)
    c2d = ctx.reshape(nbatch * ctx_len, d)
    ffn_in = [[ffn_w_in[l, k].astype(BF16) for k in range(2)] for l in range(depth)]
    ffn_out = [[ffn_w_out[l, k].astype(BF16) for k in range(2)] for l in range(depth)]

    def ffn_x(x2d, l, s, k):
        return _ffn(x2d, mod[l, :nbatch], row(norm_pre[l, s]), row(norm_post[l, s]),
                    ffn_in[l][k], ffn_out[l][k], s, FFN_TM, seq // FFN_TM)

    def ffn_c(c2d, l, s, k):
        return _ffn(c2d, mod[l, nbatch:nbatch + 1], row(norm_pre[l, s]), row(norm_post[l, s]),
                    ffn_in[l][k], ffn_out[l][k], s, ctx_len, 1)

    l = 0
    x2d = ffn_x(x2d, l, 0, 0)
    c2d = ffn_c(c2d, l, 0, 0)
    ev_args = (row(norm_pre[l, 1]), row(norm_post[l, 1]), ev_w_in[0].astype(BF16), ev_conv_w[0],
               row(ev_conv_b[0]), row(ev_ln_g[0]), row(ev_ln_b[0]), ev_w_out[0].astype(BF16))
    x2d = _fourier_conv_mix(x2d, mod[l, :nbatch], *ev_args, nbatch, ROW_TM)
    c2d = _fourier_conv_mix(c2d, mod[l, nbatch:nbatch + 1], *ev_args, nbatch, ctx_len)
    x2d = ffn_x(x2d, l, 2, 1)
    c2d = ffn_c(c2d, l, 2, 1)

    l = 1
    x2d = ffn_x(x2d, l, 0, 0)
    c2d = ffn_c(c2d, l, 0, 0)
    lam_init = 0.8 - 0.6 * math.exp(-0.3 * l)
    x2d = _diff_window_mix(x2d, c2d, mod[l, :nbatch], mod[l, nbatch:nbatch + 1],
                           row(norm_pre[l, 1]), row(norm_post[l, 1]), od_w_in[0].astype(BF16),
                           od_lambda[0], row(od_subln_g[0]), od_sink[0], od_w_out[0].astype(BF16),
                           lam_init, nbatch, ROW_TM)
    x2d = ffn_x(x2d, l, 2, 1)
    return x2d.reshape(nbatch, seq, d)
```

```python
import functools
import math

import numpy as np
import jax
import jax.numpy as jnp
from jax import lax
from jax.experimental import pallas as pl
from jax.experimental.pallas import tpu as pltpu

F32 = jnp.float32
BF16 = jnp.bfloat16

D_MODEL = 1024
DEPTH = 2
GRID_W = 64
N_MOD = 9
EPS = 1e-6
NEG_INF = -1e30
HALF_STEP = 0.5
D_FF = 2816
FOURIER_GROUPS = 4
FOURIER_GW = D_MODEL // 8
D_FOURIER = FOURIER_GROUPS * FOURIER_GW
D_CONV = D_MODEL // 2
CONV_W = 31
HEAD_DIM = 64
ROPE_BASE = 10000.0
H_DIFF = 4
DIFF_QK = H_DIFF * 2 * HEAD_DIM
DIFF_V = H_DIFF * 2 * HEAD_DIM
H_WIN = 8
H_WIN_KV = 2
WIN_G = H_WIN // H_WIN_KV
WIN_Q = H_WIN * HEAD_DIM
WIN_KV = H_WIN_KV * HEAD_DIM
WINDOW = 128
Q_BLOCK = 128

V7X_LANES = 128
V7X_MXU_DIM = 256
V7X_BF16_SUBLANES = 16
V7X_VMEM_LIMIT_BYTES = 56 * 1024 * 1024

FFN_TM = 1024
FFN_SUB = 256
FFN_FC = V7X_MXU_DIM
ROW_TM = 256
OUT_TM = 512
LOG2_E = math.log2(math.e)
KV_CHUNK = 256
FFT_RB = 8
CONV_RB = 32
CONV_HALO = V7X_BF16_SUBLANES


def _cparams(*sem):
    return pltpu.CompilerParams(dimension_semantics=sem, vmem_limit_bytes=V7X_VMEM_LIMIT_BYTES)


def _const_spec(shape):
    n = len(shape)
    return pl.BlockSpec(shape, lambda *_: (0,) * n, pipeline_mode=pl.Buffered(1))


def _dot(a, b):
    return jnp.dot(a, b, preferred_element_type=F32)


def _dot_nt(a, b):
    return lax.dot_general(a, b, (((1,), (1,)), ((), ())), preferred_element_type=F32)


def _rms(xf, g):
    return xf * lax.rsqrt(jnp.mean(xf * xf, axis=-1, keepdims=True) + EPS) * g


def _mrow(m_ref, j):
    return m_ref[0, j:j + 1, :]


def _pre(x, g, m_ref, s):
    return _rms(x, g) * (1.0 + _mrow(m_ref, 3 * s + 1)) + _mrow(m_ref, 3 * s)


def _mod_spec(mod, tiles_per_batch):
    d = mod.shape[-1]
    if mod.shape[0] == 1:
        return pl.BlockSpec((1, N_MOD, d), lambda i: (0, 0, 0))
    return pl.BlockSpec((1, N_MOD, d), lambda i: (i // tiles_per_batch, 0, 0))


def _sigmoid(x):
    return 1.0 / (1.0 + jnp.exp(-x))


def _mod_kernel(c_ref, w_ref, b_ref, o_ref):
    c = c_ref[...]
    h = c * _sigmoid(c)
    o_ref[0] = jnp.dot(h, w_ref[0], preferred_element_type=F32,
                       precision=lax.Precision.HIGHEST) + b_ref[0]


def _modulation(cs, w_mod, b_mod):
    depth, d, nd = w_mod.shape
    rows = cs.shape[0]
    return pl.pallas_call(
        _mod_kernel,
        out_shape=jax.ShapeDtypeStruct((depth, rows, nd), F32),
        grid=(depth, nd // d),
        in_specs=[pl.BlockSpec((rows, d), lambda l, j: (0, 0)),
                  pl.BlockSpec((1, d, d), lambda l, j: (l, 0, j)),
                  pl.BlockSpec((1, 1, d), lambda l, j: (l, 0, j))],
        out_specs=pl.BlockSpec((1, rows, d), lambda l, j: (l, 0, j)),
        compiler_params=_cparams("arbitrary", "arbitrary"),
        name="modulation",
    )(cs, w_mod, b_mod.reshape(depth, 1, nd))


def _ffn_kernel(s, x_ref, m_ref, gpre_ref, gpost_ref, win_ref, wout_ref, o_ref, a_ref):
    sub = min(FFN_SUB, x_ref.shape[0])
    for j in range(x_ref.shape[0] // sub):
        rows = slice(j * sub, (j + 1) * sub)
        x = x_ref[rows, :]
        h = _pre(x, gpre_ref[...], m_ref, s).astype(BF16)
        for c in range(D_FF // FFN_FC):
            lo = c * FFN_FC
            g = _dot(h, win_ref[:, lo:lo + FFN_FC])
            u = _dot(h, win_ref[:, D_FF + lo:D_FF + lo + FFN_FC])
            a_ref[rows, lo:lo + FFN_FC] = (g * _sigmoid(g) * u).astype(BF16)
        y = _dot(a_ref[rows, :], wout_ref[...])
        o_ref[rows, :] = x + HALF_STEP * _mrow(m_ref, 3 * s + 2) * _rms(y, gpost_ref[...])


def _ffn(x2d, mod, gpre, gpost, w_in, w_out, s, tm, tiles_per_batch):
    t, d = x2d.shape
    return pl.pallas_call(
        functools.partial(_ffn_kernel, s),
        out_shape=jax.ShapeDtypeStruct((t, d), F32),
        grid=(t // tm,),
        in_specs=[pl.BlockSpec((tm, d), lambda i: (i, 0)),
                  _mod_spec(mod, tiles_per_batch),
                  _const_spec((1, d)), _const_spec((1, d)),
                  _const_spec(w_in.shape), _const_spec(w_out.shape)],
        out_specs=pl.BlockSpec((tm, d), lambda i: (i, 0)),
        scratch_shapes=[pltpu.VMEM((tm, D_FF), BF16)],
        compiler_params=_cparams("arbitrary"),
        name="swiglu_halfstep",
    )(x2d, mod, gpre, gpost, w_in, w_out)


def _outproj_kernel(a_ref, b_ref, x_ref, m_ref, gpost_ref, w_ref, o_ref):
    ka = a_ref.shape[1]
    y = _dot(a_ref[...].astype(BF16), w_ref[:ka]) + _dot(b_ref[...].astype(BF16), w_ref[ka:])
    o_ref[...] = x_ref[...] + _mrow(m_ref, 5) * _rms(y, gpost_ref[...])


def _outproj(a, b, x2d, mod, gpost, w_out, tm, tiles_per_batch):
    t, d = x2d.shape
    return pl.pallas_call(
        _outproj_kernel,
        out_shape=jax.ShapeDtypeStruct((t, d), F32),
        grid=(t // tm,),
        in_specs=[pl.BlockSpec((tm, a.shape[1]), lambda i: (i, 0)),
                  pl.BlockSpec((tm, b.shape[1]), lambda i: (i, 0)),
                  pl.BlockSpec((tm, d), lambda i: (i, 0)),
                  _mod_spec(mod, tiles_per_batch),
                  _const_spec((1, d)), _const_spec(w_out.shape)],
        out_specs=pl.BlockSpec((tm, d), lambda i: (i, 0)),
        compiler_params=_cparams("arbitrary"),
        name="mixer_outproj",
    )(a, b, x2d, mod, gpost, w_out)


def _ev_proj_kernel(x_ref, m_ref, gpre_ref, w_ref, cw_ref, p_ref, u_ref):
    h = _pre(x_ref[...], gpre_ref[...], m_ref, 1).astype(BF16)
    a = _dot(h, w_ref[:, :D_FOURIER])
    g1 = _dot(h, w_ref[:, D_FOURIER:D_FOURIER + D_CONV])
    g2 = _dot(h, w_ref[:, D_FOURIER + D_CONV:])
    u_ref[...] = (g1 * _sigmoid(g2)).astype(BF16)
    gw = FOURIER_GW
    for g in range(FOURIER_GROUPS):
        pg = _dot(a[:, g * gw:(g + 1) * gw].astype(BF16), cw_ref[...])
        p_ref[0, 0, :, g * gw:(g + 1) * gw] = pg[:, :gw].astype(BF16)
        p_ref[0, 1, :, g * gw:(g + 1) * gw] = pg[:, gw:].astype(BF16)


def _ev_proj(x2d, mod, gpre, w_in, cw, nbatch, tm):
    t, d = x2d.shape
    seq = t // nbatch
    tpb = seq // tm
    return pl.pallas_call(
        _ev_proj_kernel,
        out_shape=(jax.ShapeDtypeStruct((nbatch, 2, seq, D_FOURIER), BF16),
                   jax.ShapeDtypeStruct((t, D_CONV), BF16)),
        grid=(t // tm,),
        in_specs=[pl.BlockSpec((tm, d), lambda i: (i, 0)),
                  _mod_spec(mod, tpb),
                  _const_spec((1, d)), _const_spec(w_in.shape), _const_spec(cw.shape)],
        out_specs=(pl.BlockSpec((1, 2, tm, D_FOURIER), lambda i: (i // tpb, 0, i % tpb, 0)),
                   pl.BlockSpec((tm, D_CONV), lambda i: (i, 0))),
        compiler_params=_cparams("arbitrary"),
        name="even_proj",
    )(x2d, mod, gpre, w_in, cw)


def _ev_fft_a_kernel(x_ref, m_ref, gpre_ref, w_ref, cw_ref, ka_ref, twc_ref, tws_ref, z_ref, u_ref):
    n1c = x_ref.shape[1]
    rows = n1c * FFT_RB
    x = x_ref[0].reshape(rows, D_MODEL)
    h = _pre(x, gpre_ref[...], m_ref, 1).astype(BF16)
    a = _dot(h, w_ref[:, :D_FOURIER])
    g1 = _dot(h, w_ref[:, D_FOURIER:D_FOURIER + D_CONV])
    g2 = _dot(h, w_ref[:, D_FOURIER + D_CONV:])
    u_ref[0] = (g1 * _sigmoid(g2)).reshape(n1c, FFT_RB, D_CONV)
    gw = FOURIER_GW
    pr, pi = [], []
    for g in range(FOURIER_GROUPS):
        pg = _dot(a[:, g * gw:(g + 1) * gw].astype(BF16), cw_ref[...])
        pr.append(pg[:, :gw])
        pi.append(pg[:, gw:])
    p = jnp.concatenate([jnp.concatenate(pr, axis=1), jnp.concatenate(pi, axis=1)], axis=0)
    z = _dot(ka_ref[...], p.astype(BF16))
    zr, zi = z[:rows], z[rows:]
    tc = jnp.concatenate([twc_ref[...].reshape(rows, V7X_LANES)] * FOURIER_GROUPS, axis=1)
    ts = jnp.concatenate([tws_ref[...].reshape(rows, V7X_LANES)] * FOURIER_GROUPS, axis=1)
    z_ref[0, 0] = (zr * tc + zi * ts).reshape(n1c, FFT_RB, D_FOURIER)
    z_ref[0, 1] = (zi * tc - zr * ts).reshape(n1c, FFT_RB, D_FOURIER)


def _ev_fft_a(x2d, mod, gpre, w_in, cw, ka, twc, tws, nbatch):
    t, d = x2d.shape
    seq = t // nbatch
    n1c = seq // V7X_LANES
    nrb = V7X_LANES // FFT_RB
    x4 = x2d.reshape(nbatch, n1c, V7X_LANES, d)
    blk = lambda w: pl.BlockSpec((1, n1c, FFT_RB, w), lambda i: (i // nrb, 0, i % nrb, 0))
    return pl.pallas_call(
        _ev_fft_a_kernel,
        out_shape=(jax.ShapeDtypeStruct((nbatch, 2, n1c, V7X_LANES, D_FOURIER), F32),
                   jax.ShapeDtypeStruct((nbatch, n1c, V7X_LANES, D_CONV), F32)),
        grid=(nbatch * nrb,),
        in_specs=[blk(d), _mod_spec(mod, nrb), _const_spec((1, d)), _const_spec(w_in.shape),
                  _const_spec(cw.shape), _const_spec(ka.shape),
                  pl.BlockSpec((n1c, FFT_RB, V7X_LANES), lambda i: (0, i % nrb, 0)),
                  pl.BlockSpec((n1c, FFT_RB, V7X_LANES), lambda i: (0, i % nrb, 0))],
        out_specs=(pl.BlockSpec((1, 2, n1c, FFT_RB, D_FOURIER), lambda i: (i // nrb, 0, 0, i % nrb, 0)),
                   blk(D_CONV)),
        compiler_params=_cparams("arbitrary"),
        name="even_proj_fft_a",
    )(x4, mod, gpre, w_in, cw, ka, twc, tws)


def _ev_fft_b_kernel(kb_ref, z_ref, o_ref):
    z = z_ref[0].reshape(2 * FFT_RB * V7X_LANES, D_FOURIER).astype(BF16)
    o_ref[0] = _dot(kb_ref[...], z).reshape(V7X_LANES, FFT_RB, D_FOURIER)


def _ev_fft_b(z, kb):
    nbatch, _, n1c, _, nc = z.shape
    return pl.pallas_call(
        _ev_fft_b_kernel,
        out_shape=jax.ShapeDtypeStruct((nbatch, V7X_LANES, n1c, nc), F32),
        grid=(nbatch, n1c // FFT_RB),
        in_specs=[_const_spec(kb.shape),
                  pl.BlockSpec((1, 2, FFT_RB, V7X_LANES, nc), lambda b, j: (b, 0, j, 0, 0))],
        out_specs=pl.BlockSpec((1, V7X_LANES, FFT_RB, nc), lambda b, j: (b, 0, j, 0)),
        compiler_params=_cparams("arbitrary", "arbitrary"),
        name="fft_b",
    )(kb, z)


def _dft_real_kernel(mat_ref, z_ref, o_ref):
    o_ref[0] = _dot(mat_ref[...], z_ref[0]).astype(o_ref.dtype)


def _dft_real(z, mat, nb):
    nbatch, k, n = z.shape
    m = mat.shape[0]
    return pl.pallas_call(
        _dft_real_kernel,
        out_shape=jax.ShapeDtypeStruct((nbatch, m, n), BF16),
        grid=(nbatch, n // nb),
        in_specs=[_const_spec(mat.shape), pl.BlockSpec((1, k, nb), lambda b, j: (b, 0, j))],
        out_specs=pl.BlockSpec((1, m, nb), lambda b, j: (b, 0, j)),
        compiler_params=_cparams("arbitrary", "arbitrary"),
        name="dft_real",
    )(mat, z)


def _conv_kernel(tpb, u_ref, up_ref, un_ref, cw_ref, cb_ref, lg_ref, lb_ref, o_ref, ext_ref):
    tm = u_ref.shape[0]
    j = pl.program_id(0) % tpb
    halo = CONV_HALO
    ext_ref[0:halo] = jnp.where(j > 0, up_ref[...].astype(F32), 0.0)
    ext_ref[halo:halo + tm] = u_ref[...].astype(F32)
    ext_ref[halo + tm:halo + tm + halo] = jnp.where(j < tpb - 1, un_ref[...].astype(F32), 0.0)
    cb, lg, lb = cb_ref[...], lg_ref[...], lb_ref[...]
    first = halo - CONV_W // 2

    def block(rb, carry):
        base = pl.multiple_of(rb * CONV_RB, CONV_RB)
        acc = None
        for r in range(8):
            part = None
            for t in range(r, CONV_W, 8):
                rows = ext_ref[pl.ds(pl.multiple_of(base + (t - r), 8), CONV_RB + 8), :]
                term = cw_ref[t:t + 1, :] * rows
                part = term if part is None else part + term
            part = part[first + r:first + r + CONV_RB, :]
            acc = part if acc is None else acc + part
        v = acc + cb
        vc = v - jnp.mean(v, axis=-1, keepdims=True)
        var = jnp.mean(vc * vc, axis=-1, keepdims=True)
        y = vc * lax.rsqrt(var + EPS) * lg + lb
        o_ref[pl.ds(base, CONV_RB), :] = (y * _sigmoid(y)).astype(BF16)
        return carry

    lax.fori_loop(0, tm // CONV_RB, block, 0)


def _conv_ln_swish(u, conv_w, conv_b, ln_g, ln_b, nbatch, tm):
    t, dc = u.shape
    tpb = (t // nbatch) // tm
    hb = tm // CONV_HALO
    last = t // CONV_HALO - 1
    return pl.pallas_call(
        functools.partial(_conv_kernel, tpb),
        out_shape=jax.ShapeDtypeStruct((t, dc), BF16),
        grid=(t // tm,),
        in_specs=[pl.BlockSpec((tm, dc), lambda i: (i, 0)),
                  pl.BlockSpec((CONV_HALO, dc), lambda i: (jnp.maximum(i * hb - 1, 0), 0)),
                  pl.BlockSpec((CONV_HALO, dc), lambda i: (jnp.minimum((i + 1) * hb, last), 0)),
                  _const_spec(conv_w.shape), _const_spec((1, dc)), _const_spec((1, dc)),
                  _const_spec((1, dc))],
        out_specs=pl.BlockSpec((tm, dc), lambda i: (i, 0)),
        scratch_shapes=[pltpu.VMEM((tm + 2 * CONV_HALO, dc), F32)],
        compiler_params=_cparams("arbitrary"),
        name="conv_ln_swish",
    )(u, u, u, conv_w, conv_b, ln_g, ln_b)


def _rope(z, cos, sin, even_block):
    out = []
    for j in range(z.shape[1] // V7X_LANES):
        zs = z[:, j * V7X_LANES:(j + 1) * V7X_LANES]
        sw = jnp.where(even_block, pltpu.roll(zs, V7X_LANES - 16, 1), pltpu.roll(zs, 16, 1))
        out.append(zs * cos + sw * sin)
    return out[0] if len(out) == 1 else jnp.concatenate(out, axis=1)


def _od_proj_kernel(x_ref, m_ref, gpre_ref, w_ref, cos_ref, sin_ref,
                    dq_ref, dk_ref, vt_ref, wq_ref, wk_ref, wv_ref):
    h = _pre(x_ref[...], gpre_ref[...], m_ref, 1).astype(BF16)
    cos, sin = cos_ref[...], sin_ref[...]
    lane = lax.broadcasted_iota(jnp.int32, (1, V7X_LANES), 1)
    even_block = ((lane >> 4) & 1) == 0
    scale = HEAD_DIM ** -0.5
    o = 0
    dq_ref[...] = (_rope(_dot(h, w_ref[:, o:o + DIFF_QK]), cos, sin, even_block)
                   * (scale * LOG2_E)).astype(BF16)
    o += DIFF_QK
    dk_ref[...] = _rope(_dot(h, w_ref[:, o:o + DIFF_QK]), cos, sin, even_block).astype(BF16)
    o += DIFF_QK
    vt_ref[0, 0] = _dot(h, w_ref[:, o:o + DIFF_V]).T.astype(BF16)
    o += DIFF_V
    wq_ref[...] = (_rope(_dot(h, w_ref[:, o:o + WIN_Q]), cos, sin, even_block) * scale).astype(BF16)
    o += WIN_Q
    wk_ref[...] = _rope(_dot(h, w_ref[:, o:o + WIN_KV]), cos, sin, even_block).astype(BF16)
    o += WIN_KV
    wv_ref[...] = _dot(h, w_ref[:, o:o + WIN_KV]).astype(BF16)


def _od_proj(x2d, mod, gpre, w_in, cos, sin, nbatch, tm):
    t, d = x2d.shape
    seq = t // nbatch
    tpb = seq // tm
    row = lambda i: (i, 0)
    return pl.pallas_call(
        _od_proj_kernel,
        out_shape=(jax.ShapeDtypeStruct((t, DIFF_QK), BF16),
                   jax.ShapeDtypeStruct((t, DIFF_QK), BF16),
                   jax.ShapeDtypeStruct((nbatch, tpb, DIFF_V, tm), BF16),
                   jax.ShapeDtypeStruct((t, WIN_Q), BF16),
                   jax.ShapeDtypeStruct((t, WIN_KV), BF16),
                   jax.ShapeDtypeStruct((t, WIN_KV), BF16)),
        grid=(t // tm,),
        in_specs=[pl.BlockSpec((tm, d), row),
                  _mod_spec(mod, tpb),
                  _const_spec((1, d)), _const_spec(w_in.shape),
                  pl.BlockSpec((tm, V7X_LANES), lambda i: (i % tpb, 0)),
                  pl.BlockSpec((tm, V7X_LANES), lambda i: (i % tpb, 0))],
        out_specs=(pl.BlockSpec((tm, DIFF_QK), row),
                   pl.BlockSpec((tm, DIFF_QK), row),
                   pl.BlockSpec((1, 1, DIFF_V, tm), lambda i: (i // tpb, i % tpb, 0, 0)),
                   pl.BlockSpec((tm, WIN_Q), row),
                   pl.BlockSpec((tm, WIN_KV), row),
                   pl.BlockSpec((tm, WIN_KV), row)),
        compiler_params=_cparams("arbitrary"),
        name="odd_proj",
    )(x2d, mod, gpre, w_in, cos, sin)


def _od_proj_ctx_kernel(x_ref, m_ref, gpre_ref, w_ref, dk_ref, vt_ref, wk_ref, wv_ref):
    h = _pre(x_ref[...], gpre_ref[...], m_ref, 1).astype(BF16)
    o = DIFF_QK
    dk_ref[...] = _dot(h, w_ref[:, o:o + DIFF_QK]).astype(BF16)
    o += DIFF_QK
    vt_ref[0] = _dot(h, w_ref[:, o:o + DIFF_V]).T.astype(BF16)
    o += DIFF_V + WIN_Q
    wk_ref[...] = _dot(h, w_ref[:, o:o + WIN_KV]).astype(BF16)
    o += WIN_KV
    wv_ref[...] = _dot(h, w_ref[:, o:o + WIN_KV]).astype(BF16)


def _od_proj_ctx(c2d, mod, gpre, w_in, nbatch):
    t, d = c2d.shape
    tm = t // nbatch
    row = lambda i: (i, 0)
    return pl.pallas_call(
        _od_proj_ctx_kernel,
        out_shape=(jax.ShapeDtypeStruct((t, DIFF_QK), BF16),
                   jax.ShapeDtypeStruct((nbatch, DIFF_V, tm), BF16),
                   jax.ShapeDtypeStruct((t, WIN_KV), BF16),
                   jax.ShapeDtypeStruct((t, WIN_KV), BF16)),
        grid=(nbatch,),
        in_specs=[pl.BlockSpec((tm, d), row), _const_spec(mod.shape),
                  _const_spec((1, d)), _const_spec(w_in.shape)],
        out_specs=(pl.BlockSpec((tm, DIFF_QK), row),
                   pl.BlockSpec((1, DIFF_V, tm), lambda i: (i, 0, 0)),
                   pl.BlockSpec((tm, WIN_KV), row),
                   pl.BlockSpec((tm, WIN_KV), row)),
        compiler_params=_cparams("arbitrary"),
        name="odd_proj_ctx",
    )(c2d, mod, gpre, w_in)


def _diff_attn_kernel(lam_init, q_ref, kx_ref, kc_ref, vx_ref, vc_ref, lam_ref, g_ref, o_ref,
                      k_all, v_all, sa_ref, sb_ref):
    tq = Q_BLOCK
    seq = q_ref.shape[1]
    ctx_len = kc_ref.shape[1]
    nchunk = vx_ref.shape[1] + 1
    nq = seq // tq
    k_all[0:ctx_len] = kc_ref[0]
    k_all[ctx_len:ctx_len + seq] = kx_ref[0]
    v_all[0] = vc_ref[0]
    v_all[1:nchunk] = vx_ref[0]
    lv = lam_ref[...]
    lam = (jnp.exp(jnp.sum(lv[0:1] * lv[1:2], axis=-1, keepdims=True))
           - jnp.exp(jnp.sum(lv[2:3] * lv[3:4], axis=-1, keepdims=True)) + lam_init)
    lane = lax.broadcasted_iota(jnp.int32, (1, 2 * HEAD_DIM), 1)
    first_half = lane < HEAD_DIM
    zero = jnp.zeros((), BF16)

    def scores(t, s_ref):
        r0 = pl.multiple_of(t * tq, tq)
        q = q_ref[0, pl.ds(r0, tq), :]
        qq = jnp.concatenate([jnp.where(first_half, q, zero),
                              jnp.where(first_half, zero, q)], axis=0)
        mcol = jnp.full((1, 2 * tq), NEG_INF, F32)
        for c in range(nchunk):
            s = _dot_nt(k_all[c * KV_CHUNK:(c + 1) * KV_CHUNK, :], qq)
            s_ref[c] = s
            mcol = jnp.maximum(mcol, jnp.max(s, axis=0, keepdims=True))
        return mcol

    def attend(t, s_ref, mcol):
        acc = jnp.zeros((2 * HEAD_DIM, 2 * tq), F32)
        l = jnp.zeros((1, 2 * tq), F32)
        for c in range(nchunk):
            p = jnp.exp2(s_ref[c] - mcol)
            l = l + jnp.sum(p, axis=0, keepdims=True)
            acc = acc + _dot(v_all[c], p.astype(BF16))
        o = acc / l
        d = (o[:, :tq] - lam * o[:, tq:]).T
        y = _rms(d, g_ref[...]) * (1.0 - lam_init)
        o_ref[0, pl.ds(pl.multiple_of(t * tq, tq), tq), :] = y.astype(BF16)

    def pair(i, m_a):
        t = 2 * i
        m_b = scores(t + 1, sb_ref)
        attend(t, sa_ref, m_a)
        m_a = scores(jnp.minimum(t + 2, nq - 1), sa_ref)
        attend(t + 1, sb_ref, m_b)
        return m_a

    lax.fori_loop(0, nq // 2, pair, scores(0, sa_ref))


def _diff_attn(dq, dk, vt, dkc, vtc, lam_vec, subln_g, lam_init):
    nbatch, seq, _ = dq.shape
    ctx_len = dkc.shape[1]
    nchunk = vt.shape[1]
    hd2 = 2 * HEAD_DIM
    return pl.pallas_call(
        functools.partial(_diff_attn_kernel, lam_init),
        out_shape=jax.ShapeDtypeStruct((nbatch, seq, DIFF_V), BF16),
        grid=(nbatch, H_DIFF),
        in_specs=[pl.BlockSpec((1, seq, hd2), lambda b, h: (b, 0, h)),
                  pl.BlockSpec((1, seq, hd2), lambda b, h: (b, 0, h)),
                  pl.BlockSpec((1, ctx_len, hd2), lambda b, h: (b, 0, h)),
                  pl.BlockSpec((1, nchunk, hd2, KV_CHUNK), lambda b, h: (b, 0, h, 0)),
                  pl.BlockSpec((1, hd2, ctx_len), lambda b, h: (b, h, 0)),
                  _const_spec(lam_vec.shape), _const_spec((1, hd2))],
        out_specs=pl.BlockSpec((1, seq, hd2), lambda b, h: (b, 0, h)),
        scratch_shapes=[pltpu.VMEM((ctx_len + seq, hd2), BF16),
                        pltpu.VMEM((nchunk + 1, hd2, KV_CHUNK), BF16),
                        pltpu.VMEM((nchunk + 1, KV_CHUNK, 2 * Q_BLOCK), F32),
                        pltpu.VMEM((nchunk + 1, KV_CHUNK, 2 * Q_BLOCK), F32)],
        compiler_params=_cparams("arbitrary", "arbitrary"),
        name="diff_attention",
    )(dq, dk, dkc, vt, vtc, lam_vec, subln_g)


WIN_GW = WIN_G * HEAD_DIM
WIN_REP_ROWS = 512


def _win_attn_kernel(q_ref, k_ref, v_ref, kc_ref, vc_ref, rep_ref, bias_ref, sink_ref, o_ref,
                     kr_ref, vr_ref, kcr_ref, vcr_ref, *s_refs):
    seq = q_ref.shape[1]
    ctx_len = kc_ref.shape[1]
    tq = Q_BLOCK
    nq = seq // tq
    span = tq + 2 * WINDOW
    kvh = pl.program_id(1)
    rep = rep_ref[0]

    def widen(c, carry):
        r0 = pl.multiple_of(c * WIN_REP_ROWS, WIN_REP_ROWS)
        kr_ref[pl.ds(r0, WIN_REP_ROWS), :] = _dot(k_ref[0, pl.ds(r0, WIN_REP_ROWS), :], rep).astype(BF16)
        vr_ref[pl.ds(r0, WIN_REP_ROWS), :] = _dot(v_ref[0, pl.ds(r0, WIN_REP_ROWS), :], rep).astype(BF16)
        return carry

    lax.fori_loop(0, seq // WIN_REP_ROWS, widen, 0)
    kcr_ref[...] = _dot(kc_ref[0], rep).astype(BF16)
    vcr_ref[...] = _dot(vc_ref[0], rep).astype(BF16)
    lane_group = lax.broadcasted_iota(jnp.int32, (1, WIN_GW), 1) >> 6
    zero = jnp.zeros((), BF16)
    grow = lax.broadcasted_iota(jnp.int32, (WIN_G * tq, 1), 0) >> 7
    sink = jnp.zeros((WIN_G * tq, 1), F32)
    for g in range(WIN_G):
        sink = jnp.where(grow == g, sink_ref[kvh * WIN_G + g], sink)

    def window_start(t):
        return pl.multiple_of(jnp.clip(t * tq - WINDOW, 0, seq - span), tq)

    def scores(t, s_ref):
        t = jnp.minimum(t, nq - 1)
        r0 = pl.multiple_of(t * tq, tq)
        edge = jnp.where(t == 0, 0, jnp.where(t == nq - 1, 2, 1))
        q = q_ref[0, pl.ds(r0, tq), :]
        q4 = jnp.concatenate([jnp.where(lane_group == g, q, zero) for g in range(WIN_G)], axis=0)
        s_c = _dot_nt(q4, kcr_ref[...])
        s_w = _dot_nt(q4, kr_ref[pl.ds(window_start(t), span), :]) + bias_ref[edge]
        s_ref[:, 0:ctx_len] = s_c
        s_ref[:, ctx_len:ctx_len + span] = s_w
        m = jnp.maximum(jnp.maximum(jnp.max(s_c, axis=-1, keepdims=True),
                                    jnp.max(s_w, axis=-1, keepdims=True)), sink)
        s_ref[:, ctx_len + span:] = jnp.broadcast_to(m, (WIN_G * tq, V7X_LANES))

    def attend(t, s_ref):
        m = s_ref[:, ctx_len + span:ctx_len + span + 1]
        e_c = jnp.exp(s_ref[:, 0:ctx_len] - m)
        e_w = jnp.exp(s_ref[:, ctx_len:ctx_len + span] - m)
        denom = (jnp.sum(e_c, axis=-1, keepdims=True) + jnp.sum(e_w, axis=-1, keepdims=True)
                 + jnp.exp(sink - m))
        o4 = (_dot(e_c.astype(BF16), vcr_ref[...])
              + _dot(e_w.astype(BF16), vr_ref[pl.ds(window_start(t), span), :])) / denom
        out = jnp.zeros((tq, WIN_GW), F32)
        for g in range(WIN_G):
            out = out + jnp.where(lane_group == g, o4[g * tq:(g + 1) * tq], 0.0)
        o_ref[0, pl.ds(pl.multiple_of(t * tq, tq), tq), :] = out.astype(BF16)

    def quad(i, carry):
        t = 4 * i
        scores(t + 2, s_refs[2])
        scores(t + 3, s_refs[3])
        attend(t, s_refs[0])
        attend(t + 1, s_refs[1])
        scores(t + 4, s_refs[0])
        scores(t + 5, s_refs[1])
        attend(t + 2, s_refs[2])
        attend(t + 3, s_refs[3])
        return carry

    scores(0, s_refs[0])
    scores(1, s_refs[1])
    lax.fori_loop(0, nq // 4, quad, 0)


def _win_attn(wq, wk, wv, wkc, wvc, rep, bias, sink):
    nbatch, seq, _ = wq.shape
    ctx_len = wkc.shape[1]
    assert (seq // Q_BLOCK) % 4 == 0
    span = Q_BLOCK + 2 * WINDOW
    srows, scols = WIN_G * Q_BLOCK, ctx_len + span + V7X_LANES
    per_b = lambda b, j: (b, 0, 0)
    per_bh = lambda b, j: (b, 0, j)
    return pl.pallas_call(
        _win_attn_kernel,
        out_shape=jax.ShapeDtypeStruct((nbatch, seq, WIN_Q), BF16),
        grid=(nbatch, H_WIN_KV),
        in_specs=[pl.BlockSpec((1, seq, WIN_GW), per_bh),
                  pl.BlockSpec((1, seq, WIN_KV), per_b),
                  pl.BlockSpec((1, seq, WIN_KV), per_b),
                  pl.BlockSpec((1, ctx_len, WIN_KV), per_b),
                  pl.BlockSpec((1, ctx_len, WIN_KV), per_b),
                  pl.BlockSpec((1, WIN_KV, WIN_GW), lambda b, j: (j, 0, 0)),
                  _const_spec(bias.shape),
                  pl.BlockSpec(memory_space=pltpu.SMEM)],
        out_specs=pl.BlockSpec((1, seq, WIN_GW), per_bh),
        scratch_shapes=[pltpu.VMEM((seq, WIN_GW), BF16), pltpu.VMEM((seq, WIN_GW), BF16),
                        pltpu.VMEM((ctx_len, WIN_GW), BF16), pltpu.VMEM((ctx_len, WIN_GW), BF16),
                        *[pltpu.VMEM((srows, scols), F32) for _ in range(4)]],
        compiler_params=_cparams("arbitrary", "arbitrary"),
        name="window_attention",
    )(wq, wk, wv, wkc, wvc, rep, bias, sink)


def _cos_sin(num, den):
    ang = 2.0 * np.pi * (np.asarray(num, np.int64) % den).astype(np.float64) / den
    return np.cos(ang), np.sin(ang)


def _channel_dft_table():
    k = np.arange(FOURIER_GW)
    c, s = _cos_sin(np.outer(k, k), FOURIER_GW)
    return np.concatenate([c, -s], axis=1)


def _stage_a_table(n1c):
    k = np.arange(n1c)
    c, s = _cos_sin(np.outer(k, k), n1c)
    return np.kron(np.block([[c, s], [-s, c]]), np.eye(FFT_RB))


def _stage_b_table(norm):
    k = np.arange(V7X_LANES)
    c, s = _cos_sin(np.outer(k, k), V7X_LANES)
    kb = np.zeros((V7X_LANES, FFT_RB, 2, FFT_RB, V7X_LANES))
    for i in range(FFT_RB):
        kb[:, i, 0, i, :] = c / norm
        kb[:, i, 1, i, :] = s / norm
    return kb.reshape(V7X_LANES * FFT_RB, 2 * FFT_RB * V7X_LANES)


def _twiddle_tables(seq):
    n1c = seq // V7X_LANES
    c, s = _cos_sin(np.outer(np.arange(n1c), np.arange(V7X_LANES)), seq)
    expand = lambda t: np.repeat(t[:, :, None], V7X_LANES, axis=2)
    return expand(c), expand(s)


def _real_dft_table(n, norm):
    k = np.arange(n)
    c, s = _cos_sin(np.outer(k, k), n)
    return np.concatenate([c, s], axis=1) / norm


def _rope_tables(seq):
    t = np.arange(seq)
    axis_dim = HEAD_DIM // 2
    inv_freq = ROPE_BASE ** (-np.arange(0, axis_dim, 2, dtype=np.float64) / axis_dim)
    ang_r = (t // GRID_W).astype(np.float64)[:, None] * inv_freq[None, :]
    ang_c = (t % GRID_W).astype(np.float64)[:, None] * inv_freq[None, :]
    cos = np.concatenate([np.cos(ang_r)] * 2 + [np.cos(ang_c)] * 2, axis=1)
    sin = np.concatenate([-np.sin(ang_r), np.sin(ang_r), -np.sin(ang_c), np.sin(ang_c)], axis=1)
    reps = V7X_LANES // HEAD_DIM
    return np.tile(cos, (1, reps)), np.tile(sin, (1, reps))


def _window_bias_table():
    span = Q_BLOCK + 2 * WINDOW
    q = np.arange(Q_BLOCK)[:, None]
    k = np.arange(span)[None, :]
    out = []
    for start in (0, WINDOW, 2 * WINDOW):
        vis = np.abs(k - start - q) <= WINDOW
        out.append(np.tile(np.where(vis, 0.0, NEG_INF), (WIN_G, 1)))
    return np.stack(out).astype(np.float32)


def _replication_table():
    rep = np.zeros((H_WIN_KV, WIN_KV, WIN_GW), np.float32)
    for j in range(H_WIN_KV):
        for g in range(WIN_G):
            for d in range(HEAD_DIM):
                rep[j, j * HEAD_DIM + d, g * HEAD_DIM + d] = 1.0
    return rep


def _fourier_conv_mix(h2d, mod, gpre, gpost, w_in, conv_w, conv_b, ln_g, ln_b, w_out, nbatch, tm):
    t = h2d.shape[0]
    seq = t // nbatch
    table = lambda a: jnp.asarray(a, F32).astype(BF16)
    cw = table(_channel_dft_table())
    norm = math.sqrt(seq * FOURIER_GW)
    if seq % (FFT_RB * V7X_LANES) == 0:
        n1c = seq // V7X_LANES
        twc, tws = _twiddle_tables(seq)
        z, u = _ev_fft_a(h2d, mod, gpre, w_in, cw, table(_stage_a_table(n1c)),
                         jnp.asarray(twc, F32), jnp.asarray(tws, F32), nbatch)
        fa = _ev_fft_b(z, table(_stage_b_table(norm)))
        u = u.reshape(t, D_CONV)
    else:
        p, u = _ev_proj(h2d, mod, gpre, w_in, cw, nbatch, tm)
        fa = _dft_real(p.reshape(nbatch, 2 * seq, D_FOURIER), table(_real_dft_table(seq, norm)),
                       D_FOURIER)
    fa = fa.reshape(t, D_FOURIER)
    uc = _conv_ln_swish(u, conv_w, conv_b, ln_g, ln_b, nbatch, tm)
    out_tm = math.gcd(seq, OUT_TM)
    return _outproj(fa, uc, h2d, mod, gpost, w_out, out_tm, seq // out_tm)


def _diff_window_mix(x2d, c2d, modx, modc, gpre, gpost, w_in, lam_vec, subln_g, sink, w_out,
                     lam_init, nbatch):
    t = x2d.shape[0]
    seq = t // nbatch
    ctx_len = c2d.shape[0] // nbatch
    cos, sin = _rope_tables(seq)
    dq, dk, vt, wq, wk, wv = _od_proj(x2d, modx, gpre, w_in, jnp.asarray(cos, F32),
                                      jnp.asarray(sin, F32), nbatch, KV_CHUNK)
    dkc, vtc, wkc, wvc = _od_proj_ctx(c2d, modc, gpre, w_in, nbatch)
    b3 = lambda a, n: a.reshape(nbatch, n, a.shape[-1])
    diff = _diff_attn(b3(dq, seq), b3(dk, seq), vt, b3(dkc, ctx_len), vtc, lam_vec, subln_g, lam_init)
    win = _win_attn(b3(wq, seq), b3(wk, seq), b3(wv, seq), b3(wkc, ctx_len), b3(wvc, ctx_len),
                    jnp.asarray(_replication_table(), F32).astype(BF16),
                    jnp.asarray(_window_bias_table(), F32), sink)
    out_tm = math.gcd(seq, OUT_TM)
    return _outproj(diff.reshape(t, DIFF_V), win.reshape(t, WIN_Q), x2d, modx, gpost, w_out,
                    out_tm, seq // out_tm)


def kernel(x, c, ctx, c_ctx, w_mod, b_mod, norm_pre, norm_post, ffn_w_in, ffn_w_out, ev_w_in, ev_conv_w, ev_conv_b, ev_ln_g, ev_ln_b, ev_w_out, od_w_in, od_lambda, od_subln_g, od_sink, od_w_out):
    nbatch, seq, d = x.shape
    ctx_len = ctx.shape[1]
    depth = w_mod.shape[0]
    assert d == D_MODEL and depth == DEPTH == 2 and ctx_len == KV_CHUNK
    assert seq % FFN_TM == 0 and seq % (8 * V7X_LANES) == 0

    mod_rows = 8
    cs = jnp.concatenate([c, c_ctx[None, :], jnp.zeros((mod_rows - nbatch - 1, d), F32)], axis=0)
    mod = _modulation(cs, w_mod, b_mod).reshape(depth, mod_rows, N_MOD, d)
    row = lambda v: v.reshape(1, -1)

    x2d = x.reshape(nbatch * seq, d)
    c2d = ctx.reshape(nbatch * ctx_len, d)
    ffn_in = [[ffn_w_in[l, k].astype(BF16) for k in range(2)] for l in range(depth)]
    ffn_out = [[ffn_w_out[l, k].astype(BF16) for k in range(2)] for l in range(depth)]

    def ffn_x(x2d, l, s, k):
        return _ffn(x2d, mod[l, :nbatch], row(norm_pre[l, s]), row(norm_post[l, s]),
                    ffn_in[l][k], ffn_out[l][k], s, FFN_TM, seq // FFN_TM)

    def ffn_c(c2d, l, s, k):
        return _ffn(c2d, mod[l, nbatch:nbatch + 1], row(norm_pre[l, s]), row(norm_post[l, s]),
                    ffn_in[l][k], ffn_out[l][k], s, ctx_len, 1)

    l = 0
    x2d = ffn_x(x2d, l, 0, 0)
    c2d = ffn_c(c2d, l, 0, 0)
    ev_args = (row(norm_pre[l, 1]), row(norm_post[l, 1]), ev_w_in[0].astype(BF16), ev_conv_w[0],
               row(ev_conv_b[0]), row(ev_ln_g[0]), row(ev_ln_b[0]), ev_w_out[0].astype(BF16))
    x2d = _fourier_conv_mix(x2d, mod[l, :nbatch], *ev_args, nbatch, ROW_TM)
    c2d = _fourier_conv_mix(c2d, mod[l, nbatch:nbatch + 1], *ev_args, nbatch, ctx_len)
    x2d = ffn_x(x2d, l, 2, 1)
    c2d = ffn_c(c2d, l, 2, 1)

    l = 1
    x2d = ffn_x(x2d, l, 0, 0)
    c2d = ffn_c(c2d, l, 0, 0)
    lam_init = 0.8 - 0.6 * math.exp(-0.3 * l)
    x2d = _diff_window_mix(x2d, c2d, mod[l, :nbatch], mod[l, nbatch:nbatch + 1],
                           row(norm_pre[l, 1]), row(norm_post[l, 1]), od_w_in[0].astype(BF16),
                           od_lambda[0], row(od_subln_g[0]), od_sink[0], od_w_out[0].astype(BF16),
                           lam_init, nbatch)
    x2d = ffn_x(x2d, l, 2, 1)
    return x2d.reshape(nbatch, seq, d)
```

```python
import functools
import math

import numpy as np
import jax
import jax.numpy as jnp
from jax import lax
from jax.experimental import pallas as pl
from jax.experimental.pallas import tpu as pltpu

F32 = jnp.float32
BF16 = jnp.bfloat16

D_MODEL = 1024
DEPTH = 2
GRID_W = 64
N_MOD = 9
EPS = 1e-6
NEG_INF = -1e30
HALF_STEP = 0.5
D_FF = 2816
FOURIER_GROUPS = 4
FOURIER_GW = D_MODEL // 8
D_FOURIER = FOURIER_GROUPS * FOURIER_GW
D_CONV = D_MODEL // 2
CONV_W = 31
HEAD_DIM = 64
ROPE_BASE = 10000.0
H_DIFF = 4
DIFF_QK = H_DIFF * 2 * HEAD_DIM
DIFF_V = H_DIFF * 2 * HEAD_DIM
H_WIN = 8
H_WIN_KV = 2
WIN_G = H_WIN // H_WIN_KV
WIN_Q = H_WIN * HEAD_DIM
WIN_KV = H_WIN_KV * HEAD_DIM
WINDOW = 128
Q_BLOCK = 128

V7X_LANES = 128
V7X_MXU_DIM = 256
V7X_BF16_SUBLANES = 16
V7X_VMEM_LIMIT_BYTES = 56 * 1024 * 1024

FFN_TM = 1024
FFN_SUB = 256
FFN_FC = V7X_MXU_DIM
ROW_TM = 256
OUT_TM = 512
LOG2_E = math.log2(math.e)
KV_CHUNK = 256
FFT_RB = 8
CONV_RB = 32
CONV_HALO = V7X_BF16_SUBLANES


def _cparams(*sem):
    return pltpu.CompilerParams(dimension_semantics=sem, vmem_limit_bytes=V7X_VMEM_LIMIT_BYTES)


def _const_spec(shape):
    n = len(shape)
    return pl.BlockSpec(shape, lambda *_: (0,) * n, pipeline_mode=pl.Buffered(1))


def _dot(a, b):
    return jnp.dot(a, b, preferred_element_type=F32)


def _dot_nt(a, b):
    return lax.dot_general(a, b, (((1,), (1,)), ((), ())), preferred_element_type=F32)


def _rms(xf, g):
    return xf * lax.rsqrt(jnp.mean(xf * xf, axis=-1, keepdims=True) + EPS) * g


def _mrow(m_ref, j):
    return m_ref[0, j:j + 1, :]


def _pre(x, g, m_ref, s):
    return _rms(x, g) * (1.0 + _mrow(m_ref, 3 * s + 1)) + _mrow(m_ref, 3 * s)


def _mod_spec(mod, tiles_per_batch):
    d = mod.shape[-1]
    if mod.shape[0] == 1:
        return pl.BlockSpec((1, N_MOD, d), lambda i: (0, 0, 0))
    return pl.BlockSpec((1, N_MOD, d), lambda i: (i // tiles_per_batch, 0, 0))


def _sigmoid(x):
    return 1.0 / (1.0 + jnp.exp(-x))


def _mod_kernel(c_ref, w_ref, b_ref, o_ref):
    c = c_ref[...]
    h = c * _sigmoid(c)
    o_ref[0] = jnp.dot(h, w_ref[0], preferred_element_type=F32,
                       precision=lax.Precision.HIGHEST) + b_ref[0]


def _modulation(cs, w_mod, b_mod):
    depth, d, nd = w_mod.shape
    rows = cs.shape[0]
    return pl.pallas_call(
        _mod_kernel,
        out_shape=jax.ShapeDtypeStruct((depth, rows, nd), F32),
        grid=(depth, nd // d),
        in_specs=[pl.BlockSpec((rows, d), lambda l, j: (0, 0)),
                  pl.BlockSpec((1, d, d), lambda l, j: (l, 0, j)),
                  pl.BlockSpec((1, 1, d), lambda l, j: (l, 0, j))],
        out_specs=pl.BlockSpec((1, rows, d), lambda l, j: (l, 0, j)),
        compiler_params=_cparams("arbitrary", "arbitrary"),
        name="modulation",
    )(cs, w_mod, b_mod.reshape(depth, 1, nd))


def _ffn_kernel(s, x_ref, m_ref, gpre_ref, gpost_ref, win_ref, wout_ref, o_ref, a_ref):
    sub = min(FFN_SUB, x_ref.shape[0])
    for j in range(x_ref.shape[0] // sub):
        rows = slice(j * sub, (j + 1) * sub)
        x = x_ref[rows, :]
        h = _pre(x, gpre_ref[...], m_ref, s).astype(BF16)
        for c in range(D_FF // FFN_FC):
            lo = c * FFN_FC
            g = _dot(h, win_ref[:, lo:lo + FFN_FC])
            u = _dot(h, win_ref[:, D_FF + lo:D_FF + lo + FFN_FC])
            a_ref[rows, lo:lo + FFN_FC] = (g * _sigmoid(g) * u).astype(BF16)
        y = _dot(a_ref[rows, :], wout_ref[...])
        o_ref[rows, :] = x + HALF_STEP * _mrow(m_ref, 3 * s + 2) * _rms(y, gpost_ref[...])


def _stacked_weight_spec(w, lk):
    l, k = lk
    return pl.BlockSpec((None, None) + w.shape[2:], lambda i: (l, k, 0, 0),
                        pipeline_mode=pl.Buffered(1))


def _ffn(x2d, mod, gpre, gpost, w_in, w_out, lk, s, tm, tiles_per_batch):
    t, d = x2d.shape
    return pl.pallas_call(
        functools.partial(_ffn_kernel, s),
        out_shape=jax.ShapeDtypeStruct((t, d), F32),
        grid=(t // tm,),
        in_specs=[pl.BlockSpec((tm, d), lambda i: (i, 0)),
                  _mod_spec(mod, tiles_per_batch),
                  _const_spec((1, d)), _const_spec((1, d)),
                  _stacked_weight_spec(w_in, lk), _stacked_weight_spec(w_out, lk)],
        out_specs=pl.BlockSpec((tm, d), lambda i: (i, 0)),
        scratch_shapes=[pltpu.VMEM((tm, D_FF), BF16)],
        compiler_params=_cparams("arbitrary"),
        name="swiglu_halfstep",
    )(x2d, mod, gpre, gpost, w_in, w_out)


def _outproj_kernel(a_ref, b_ref, x_ref, m_ref, gpost_ref, w_ref, o_ref):
    ka = a_ref.shape[1]
    y = _dot(a_ref[...].astype(BF16), w_ref[:ka]) + _dot(b_ref[...].astype(BF16), w_ref[ka:])
    o_ref[...] = x_ref[...] + _mrow(m_ref, 5) * _rms(y, gpost_ref[...])


def _outproj(a, b, x2d, mod, gpost, w_out, tm, tiles_per_batch):
    t, d = x2d.shape
    return pl.pallas_call(
        _outproj_kernel,
        out_shape=jax.ShapeDtypeStruct((t, d), F32),
        grid=(t // tm,),
        in_specs=[pl.BlockSpec((tm, a.shape[1]), lambda i: (i, 0)),
                  pl.BlockSpec((tm, b.shape[1]), lambda i: (i, 0)),
                  pl.BlockSpec((tm, d), lambda i: (i, 0)),
                  _mod_spec(mod, tiles_per_batch),
                  _const_spec((1, d)), _const_spec(w_out.shape)],
        out_specs=pl.BlockSpec((tm, d), lambda i: (i, 0)),
        compiler_params=_cparams("arbitrary"),
        name="mixer_outproj",
    )(a, b, x2d, mod, gpost, w_out)


def _ev_proj_kernel(x_ref, m_ref, gpre_ref, w_ref, cw_ref, p_ref, u_ref):
    h = _pre(x_ref[...], gpre_ref[...], m_ref, 1).astype(BF16)
    a = _dot(h, w_ref[:, :D_FOURIER])
    g1 = _dot(h, w_ref[:, D_FOURIER:D_FOURIER + D_CONV])
    g2 = _dot(h, w_ref[:, D_FOURIER + D_CONV:])
    u_ref[...] = (g1 * _sigmoid(g2)).astype(BF16)
    gw = FOURIER_GW
    for g in range(FOURIER_GROUPS):
        pg = _dot(a[:, g * gw:(g + 1) * gw].astype(BF16), cw_ref[...])
        p_ref[0, 0, :, g * gw:(g + 1) * gw] = pg[:, :gw].astype(BF16)
        p_ref[0, 1, :, g * gw:(g + 1) * gw] = pg[:, gw:].astype(BF16)


def _ev_proj(x2d, mod, gpre, w_in, cw, nbatch, tm):
    t, d = x2d.shape
    seq = t // nbatch
    tpb = seq // tm
    return pl.pallas_call(
        _ev_proj_kernel,
        out_shape=(jax.ShapeDtypeStruct((nbatch, 2, seq, D_FOURIER), BF16),
                   jax.ShapeDtypeStruct((t, D_CONV), BF16)),
        grid=(t // tm,),
        in_specs=[pl.BlockSpec((tm, d), lambda i: (i, 0)),
                  _mod_spec(mod, tpb),
                  _const_spec((1, d)), _const_spec(w_in.shape), _const_spec(cw.shape)],
        out_specs=(pl.BlockSpec((1, 2, tm, D_FOURIER), lambda i: (i // tpb, 0, i % tpb, 0)),
                   pl.BlockSpec((tm, D_CONV), lambda i: (i, 0))),
        compiler_params=_cparams("arbitrary"),
        name="even_proj",
    )(x2d, mod, gpre, w_in, cw)


def _ev_fft_a_kernel(x_ref, m_ref, gpre_ref, w_ref, cw_ref, ka_ref, twc_ref, tws_ref, z_ref, u_ref):
    n1c = x_ref.shape[1]
    rows = n1c * FFT_RB
    x = x_ref[0].reshape(rows, D_MODEL)
    h = _pre(x, gpre_ref[...], m_ref, 1).astype(BF16)
    a = _dot(h, w_ref[:, :D_FOURIER])
    g1 = _dot(h, w_ref[:, D_FOURIER:D_FOURIER + D_CONV])
    g2 = _dot(h, w_ref[:, D_FOURIER + D_CONV:])
    u_ref[0] = (g1 * _sigmoid(g2)).reshape(n1c, FFT_RB, D_CONV)
    gw = FOURIER_GW
    pr, pi = [], []
    for g in range(FOURIER_GROUPS):
        pg = _dot(a[:, g * gw:(g + 1) * gw].astype(BF16), cw_ref[...])
        pr.append(pg[:, :gw])
        pi.append(pg[:, gw:])
    p = jnp.concatenate([jnp.concatenate(pr, axis=1), jnp.concatenate(pi, axis=1)], axis=0)
    z = _dot(ka_ref[...], p.astype(BF16))
    zr, zi = z[:rows], z[rows:]
    tc = jnp.concatenate([twc_ref[...].reshape(rows, V7X_LANES)] * FOURIER_GROUPS, axis=1)
    ts = jnp.concatenate([tws_ref[...].reshape(rows, V7X_LANES)] * FOURIER_GROUPS, axis=1)
    z_ref[0, 0] = (zr * tc + zi * ts).reshape(n1c, FFT_RB, D_FOURIER)
    z_ref[0, 1] = (zi * tc - zr * ts).reshape(n1c, FFT_RB, D_FOURIER)


def _ev_fft_a(x2d, mod, gpre, w_in, cw, ka, twc, tws, nbatch):
    t, d = x2d.shape
    seq = t // nbatch
    n1c = seq // V7X_LANES
    nrb = V7X_LANES // FFT_RB
    x4 = x2d.reshape(nbatch, n1c, V7X_LANES, d)
    blk = lambda w: pl.BlockSpec((1, n1c, FFT_RB, w), lambda i: (i // nrb, 0, i % nrb, 0))
    return pl.pallas_call(
        _ev_fft_a_kernel,
        out_shape=(jax.ShapeDtypeStruct((nbatch, 2, n1c, V7X_LANES, D_FOURIER), F32),
                   jax.ShapeDtypeStruct((nbatch, n1c, V7X_LANES, D_CONV), F32)),
        grid=(nbatch * nrb,),
        in_specs=[blk(d), _mod_spec(mod, nrb), _const_spec((1, d)), _const_spec(w_in.shape),
                  _const_spec(cw.shape), _const_spec(ka.shape),
                  pl.BlockSpec((n1c, FFT_RB, V7X_LANES), lambda i: (0, i % nrb, 0)),
                  pl.BlockSpec((n1c, FFT_RB, V7X_LANES), lambda i: (0, i % nrb, 0))],
        out_specs=(pl.BlockSpec((1, 2, n1c, FFT_RB, D_FOURIER), lambda i: (i // nrb, 0, 0, i % nrb, 0)),
                   blk(D_CONV)),
        compiler_params=_cparams("arbitrary"),
        name="even_proj_fft_a",
    )(x4, mod, gpre, w_in, cw, ka, twc, tws)


def _ev_fft_b_kernel(kb_ref, z_ref, o_ref):
    z = z_ref[0].reshape(2 * FFT_RB * V7X_LANES, D_FOURIER).astype(BF16)
    o_ref[0] = _dot(kb_ref[...], z).reshape(V7X_LANES, FFT_RB, D_FOURIER)


def _ev_fft_b(z, kb):
    nbatch, _, n1c, _, nc = z.shape
    return pl.pallas_call(
        _ev_fft_b_kernel,
        out_shape=jax.ShapeDtypeStruct((nbatch, V7X_LANES, n1c, nc), F32),
        grid=(nbatch, n1c // FFT_RB),
        in_specs=[_const_spec(kb.shape),
                  pl.BlockSpec((1, 2, FFT_RB, V7X_LANES, nc), lambda b, j: (b, 0, j, 0, 0))],
        out_specs=pl.BlockSpec((1, V7X_LANES, FFT_RB, nc), lambda b, j: (b, 0, j, 0)),
        compiler_params=_cparams("arbitrary", "arbitrary"),
        name="fft_b",
    )(kb, z)


def _dft_real_kernel(mat_ref, z_ref, o_ref):
    o_ref[0] = _dot(mat_ref[...], z_ref[0]).astype(o_ref.dtype)


def _dft_real(z, mat, nb):
    nbatch, k, n = z.shape
    m = mat.shape[0]
    return pl.pallas_call(
        _dft_real_kernel,
        out_shape=jax.ShapeDtypeStruct((nbatch, m, n), BF16),
        grid=(nbatch, n // nb),
        in_specs=[_const_spec(mat.shape), pl.BlockSpec((1, k, nb), lambda b, j: (b, 0, j))],
        out_specs=pl.BlockSpec((1, m, nb), lambda b, j: (b, 0, j)),
        compiler_params=_cparams("arbitrary", "arbitrary"),
        name="dft_real",
    )(mat, z)


def _conv_kernel(tpb, u_ref, up_ref, un_ref, cw_ref, cb_ref, lg_ref, lb_ref, o_ref, ext_ref):
    tm = u_ref.shape[0]
    j = pl.program_id(0) % tpb
    halo = CONV_HALO
    ext_ref[0:halo] = jnp.where(j > 0, up_ref[...].astype(F32), 0.0)
    ext_ref[halo:halo + tm] = u_ref[...].astype(F32)
    ext_ref[halo + tm:halo + tm + halo] = jnp.where(j < tpb - 1, un_ref[...].astype(F32), 0.0)
    cb, lg, lb = cb_ref[...], lg_ref[...], lb_ref[...]
    first = halo - CONV_W // 2

    def block(rb, carry):
        base = pl.multiple_of(rb * CONV_RB, CONV_RB)
        acc = None
        for r in range(8):
            part = None
            for t in range(r, CONV_W, 8):
                rows = ext_ref[pl.ds(pl.multiple_of(base + (t - r), 8), CONV_RB + 8), :]
                term = cw_ref[t:t + 1, :] * rows
                part = term if part is None else part + term
            part = part[first + r:first + r + CONV_RB, :]
            acc = part if acc is None else acc + part
        v = acc + cb
        vc = v - jnp.mean(v, axis=-1, keepdims=True)
        var = jnp.mean(vc * vc, axis=-1, keepdims=True)
        y = vc * lax.rsqrt(var + EPS) * lg + lb
        o_ref[pl.ds(base, CONV_RB), :] = (y * _sigmoid(y)).astype(BF16)
        return carry

    lax.fori_loop(0, tm // CONV_RB, block, 0)


def _conv_ln_swish(u, conv_w, conv_b, ln_g, ln_b, nbatch, tm):
    t, dc = u.shape
    tpb = (t // nbatch) // tm
    hb = tm // CONV_HALO
    last = t // CONV_HALO - 1
    return pl.pallas_call(
        functools.partial(_conv_kernel, tpb),
        out_shape=jax.ShapeDtypeStruct((t, dc), BF16),
        grid=(t // tm,),
        in_specs=[pl.BlockSpec((tm, dc), lambda i: (i, 0)),
                  pl.BlockSpec((CONV_HALO, dc), lambda i: (jnp.maximum(i * hb - 1, 0), 0)),
                  pl.BlockSpec((CONV_HALO, dc), lambda i: (jnp.minimum((i + 1) * hb, last), 0)),
                  _const_spec(conv_w.shape), _const_spec((1, dc)), _const_spec((1, dc)),
                  _const_spec((1, dc))],
        out_specs=pl.BlockSpec((tm, dc), lambda i: (i, 0)),
        scratch_shapes=[pltpu.VMEM((tm + 2 * CONV_HALO, dc), F32)],
        compiler_params=_cparams("arbitrary"),
        name="conv_ln_swish",
    )(u, u, u, conv_w, conv_b, ln_g, ln_b)


def _rope(z, cos, sin, even_block):
    out = []
    for j in range(z.shape[1] // V7X_LANES):
        zs = z[:, j * V7X_LANES:(j + 1) * V7X_LANES]
        sw = jnp.where(even_block, pltpu.roll(zs, V7X_LANES - 16, 1), pltpu.roll(zs, 16, 1))
        out.append(zs * cos + sw * sin)
    return out[0] if len(out) == 1 else jnp.concatenate(out, axis=1)


def _od_proj_kernel(x_ref, m_ref, gpre_ref, w_ref, cos_ref, sin_ref,
                    dq_ref, dk_ref, vt_ref, wq_ref, wk_ref, wv_ref):
    h = _pre(x_ref[...], gpre_ref[...], m_ref, 1).astype(BF16)
    cos, sin = cos_ref[...], sin_ref[...]
    lane = lax.broadcasted_iota(jnp.int32, (1, V7X_LANES), 1)
    even_block = ((lane >> 4) & 1) == 0
    scale = HEAD_DIM ** -0.5
    o = 0
    dq_ref[...] = (_rope(_dot(h, w_ref[:, o:o + DIFF_QK]), cos, sin, even_block)
                   * (scale * LOG2_E)).astype(BF16)
    o += DIFF_QK
    dk_ref[...] = _rope(_dot(h, w_ref[:, o:o + DIFF_QK]), cos, sin, even_block).astype(BF16)
    o += DIFF_QK
    vt_ref[0] = _dot(h, w_ref[:, o:o + DIFF_V]).T.astype(BF16)
    o += DIFF_V
    wq_ref[...] = (_rope(_dot(h, w_ref[:, o:o + WIN_Q]), cos, sin, even_block) * scale).astype(BF16)
    o += WIN_Q
    wk_ref[...] = _rope(_dot(h, w_ref[:, o:o + WIN_KV]), cos, sin, even_block).astype(BF16)
    o += WIN_KV
    wv_ref[...] = _dot(h, w_ref[:, o:o + WIN_KV]).astype(BF16)


def _od_proj(x2d, mod, gpre, w_in, cos, sin, nbatch, tm):
    t, d = x2d.shape
    seq = t // nbatch
    tpb = seq // tm
    row = lambda i: (i, 0)
    return pl.pallas_call(
        _od_proj_kernel,
        out_shape=(jax.ShapeDtypeStruct((t, DIFF_QK), BF16),
                   jax.ShapeDtypeStruct((t, DIFF_QK), BF16),
                   jax.ShapeDtypeStruct((nbatch, DIFF_V, seq), BF16),
                   jax.ShapeDtypeStruct((t, WIN_Q), BF16),
                   jax.ShapeDtypeStruct((t, WIN_KV), BF16),
                   jax.ShapeDtypeStruct((t, WIN_KV), BF16)),
        grid=(t // tm,),
        in_specs=[pl.BlockSpec((tm, d), row),
                  _mod_spec(mod, tpb),
                  _const_spec((1, d)), _const_spec(w_in.shape),
                  pl.BlockSpec((tm, V7X_LANES), lambda i: (i % tpb, 0)),
                  pl.BlockSpec((tm, V7X_LANES), lambda i: (i % tpb, 0))],
        out_specs=(pl.BlockSpec((tm, DIFF_QK), row),
                   pl.BlockSpec((tm, DIFF_QK), row),
                   pl.BlockSpec((1, DIFF_V, tm), lambda i: (i // tpb, 0, i % tpb)),
                   pl.BlockSpec((tm, WIN_Q), row),
                   pl.BlockSpec((tm, WIN_KV), row),
                   pl.BlockSpec((tm, WIN_KV), row)),
        compiler_params=_cparams("arbitrary"),
        name="odd_proj",
    )(x2d, mod, gpre, w_in, cos, sin)


def _od_proj_ctx_kernel(x_ref, m_ref, gpre_ref, w_ref, dk_ref, vt_ref, wk_ref, wv_ref):
    h = _pre(x_ref[...], gpre_ref[...], m_ref, 1).astype(BF16)
    o = DIFF_QK
    dk_ref[...] = _dot(h, w_ref[:, o:o + DIFF_QK]).astype(BF16)
    o += DIFF_QK
    vt_ref[0] = _dot(h, w_ref[:, o:o + DIFF_V]).T.astype(BF16)
    o += DIFF_V + WIN_Q
    wk_ref[...] = _dot(h, w_ref[:, o:o + WIN_KV]).astype(BF16)
    o += WIN_KV
    wv_ref[...] = _dot(h, w_ref[:, o:o + WIN_KV]).astype(BF16)


def _od_proj_ctx(c2d, mod, gpre, w_in, nbatch):
    t, d = c2d.shape
    tm = t // nbatch
    row = lambda i: (i, 0)
    return pl.pallas_call(
        _od_proj_ctx_kernel,
        out_shape=(jax.ShapeDtypeStruct((t, DIFF_QK), BF16),
                   jax.ShapeDtypeStruct((nbatch, DIFF_V, tm), BF16),
                   jax.ShapeDtypeStruct((t, WIN_KV), BF16),
                   jax.ShapeDtypeStruct((t, WIN_KV), BF16)),
        grid=(nbatch,),
        in_specs=[pl.BlockSpec((tm, d), row), _const_spec(mod.shape),
                  _const_spec((1, d)), _const_spec(w_in.shape)],
        out_specs=(pl.BlockSpec((tm, DIFF_QK), row),
                   pl.BlockSpec((1, DIFF_V, tm), lambda i: (i, 0, 0)),
                   pl.BlockSpec((tm, WIN_KV), row),
                   pl.BlockSpec((tm, WIN_KV), row)),
        compiler_params=_cparams("arbitrary"),
        name="odd_proj_ctx",
    )(c2d, mod, gpre, w_in)


def _diff_attn_kernel(lam_init, q_ref, kx_ref, kc_ref, vx_ref, vc_ref, lam_ref, g_ref, o_ref,
                      k_all, v_all, sa_ref, sb_ref, sc_ref):
    tq = Q_BLOCK
    seq = q_ref.shape[1]
    ctx_len = kc_ref.shape[1]
    nkeys = ctx_len + seq
    nq = seq // tq
    hd2 = 2 * HEAD_DIM
    k_all[0:ctx_len] = kc_ref[0]
    k_all[ctx_len:nkeys] = kx_ref[0]
    v_all[0:hd2, 0:ctx_len] = vc_ref[0]
    v_all[0:hd2, ctx_len:nkeys] = vx_ref[0]
    v_all[hd2:, :] = jnp.ones((V7X_BF16_SUBLANES, nkeys), BF16)
    lv = lam_ref[...]
    lam = (jnp.exp(jnp.sum(lv[0:1] * lv[1:2], axis=-1, keepdims=True))
           - jnp.exp(jnp.sum(lv[2:3] * lv[3:4], axis=-1, keepdims=True)) + lam_init)
    lane = lax.broadcasted_iota(jnp.int32, (1, hd2), 1)
    first_half = lane < HEAD_DIM
    zero = jnp.zeros((), BF16)

    def scores(t, s_ref):
        r0 = pl.multiple_of(t * tq, tq)
        q = q_ref[0, pl.ds(r0, tq), :]
        qq = jnp.concatenate([jnp.where(first_half, q, zero),
                              jnp.where(first_half, zero, q)], axis=0)
        s = _dot_nt(k_all[...], qq)
        s_ref[...] = s
        return jnp.max(s, axis=0, keepdims=True)

    def attend(t, s_ref, mcol):
        p = jnp.exp2((s_ref[...] - mcol).astype(BF16))
        acc = _dot(v_all[...], p)
        o = acc[:hd2] / acc[hd2:hd2 + 1]
        d = (o[:, :tq] - lam * o[:, tq:]).T
        y = _rms(d, g_ref[...]) * (1.0 - lam_init)
        o_ref[0, pl.ds(pl.multiple_of(t * tq, tq), tq), :] = y.astype(BF16)

    def triple(i, carry):
        m_a, m_b = carry
        t = 3 * i
        m_c = scores(t + 2, sc_ref)
        attend(t, sa_ref, m_a)
        m_a = scores(t + 3, sa_ref)
        attend(t + 1, sb_ref, m_b)
        m_b = scores(jnp.minimum(t + 4, nq - 1), sb_ref)
        attend(t + 2, sc_ref, m_c)
        return m_a, m_b

    m_a, _ = lax.fori_loop(0, (nq - 1) // 3, triple, (scores(0, sa_ref), scores(1, sb_ref)))
    attend(nq - 1, sa_ref, m_a)


def _diff_attn(dq, dk, vt, dkc, vtc, lam_vec, subln_g, lam_init):
    nbatch, seq, _ = dq.shape
    ctx_len = dkc.shape[1]
    nkeys = ctx_len + seq
    hd2 = 2 * HEAD_DIM
    assert (seq // Q_BLOCK - 1) % 3 == 0
    return pl.pallas_call(
        functools.partial(_diff_attn_kernel, lam_init),
        out_shape=jax.ShapeDtypeStruct((nbatch, seq, DIFF_V), BF16),
        grid=(nbatch, H_DIFF),
        in_specs=[pl.BlockSpec((1, seq, hd2), lambda b, h: (b, 0, h)),
                  pl.BlockSpec((1, seq, hd2), lambda b, h: (b, 0, h)),
                  pl.BlockSpec((1, ctx_len, hd2), lambda b, h: (b, 0, h)),
                  pl.BlockSpec((1, hd2, seq), lambda b, h: (b, h, 0)),
                  pl.BlockSpec((1, hd2, ctx_len), lambda b, h: (b, h, 0)),
                  _const_spec(lam_vec.shape), _const_spec((1, hd2))],
        out_specs=pl.BlockSpec((1, seq, hd2), lambda b, h: (b, 0, h)),
        scratch_shapes=[pltpu.VMEM((nkeys, hd2), BF16),
                        pltpu.VMEM((hd2 + V7X_BF16_SUBLANES, nkeys), BF16),
                        *[pltpu.VMEM((nkeys, 2 * Q_BLOCK), F32) for _ in range(3)]],
        compiler_params=_cparams("arbitrary", "arbitrary"),
        name="diff_attention",
    )(dq, dk, dkc, vt, vtc, lam_vec, subln_g)


WIN_GW = WIN_G * HEAD_DIM
WIN_REP_ROWS = 512


def _win_attn_kernel(q_ref, k_ref, v_ref, kc_ref, vc_ref, rep_ref, bias_ref, sink_ref, o_ref,
                     kr_ref, vr_ref, kcr_ref, vcr_ref, *s_refs):
    seq = q_ref.shape[1]
    ctx_len = kc_ref.shape[1]
    tq = Q_BLOCK
    nq = seq // tq
    span = tq + 2 * WINDOW
    kvh = pl.program_id(1)
    rep = rep_ref[0]

    def widen(c, carry):
        r0 = pl.multiple_of(c * WIN_REP_ROWS, WIN_REP_ROWS)
        kr_ref[pl.ds(r0, WIN_REP_ROWS), :] = _dot(k_ref[0, pl.ds(r0, WIN_REP_ROWS), :], rep).astype(BF16)
        vr_ref[pl.ds(r0, WIN_REP_ROWS), :] = _dot(v_ref[0, pl.ds(r0, WIN_REP_ROWS), :], rep).astype(BF16)
        return carry

    lax.fori_loop(0, seq // WIN_REP_ROWS, widen, 0)
    kcr_ref[...] = _dot(kc_ref[0], rep).astype(BF16)
    vcr_ref[...] = _dot(vc_ref[0], rep).astype(BF16)
    lane_group = lax.broadcasted_iota(jnp.int32, (1, WIN_GW), 1) >> 6
    zero = jnp.zeros((), BF16)
    grow = lax.broadcasted_iota(jnp.int32, (WIN_G * tq, 1), 0) >> 7
    sink = jnp.zeros((WIN_G * tq, 1), F32)
    for g in range(WIN_G):
        sink = jnp.where(grow == g, sink_ref[kvh * WIN_G + g], sink)

    def window_start(t):
        return pl.multiple_of(jnp.clip(t * tq - WINDOW, 0, seq - span), tq)

    def scores(t, s_ref):
        t = jnp.minimum(t, nq - 1)
        r0 = pl.multiple_of(t * tq, tq)
        edge = jnp.where(t == 0, 0, jnp.where(t == nq - 1, 2, 1))
        q = q_ref[0, pl.ds(r0, tq), :]
        q4 = jnp.concatenate([jnp.where(lane_group == g, q, zero) for g in range(WIN_G)], axis=0)
        s_c = _dot_nt(q4, kcr_ref[...])
        s_w = _dot_nt(q4, kr_ref[pl.ds(window_start(t), span), :]) + bias_ref[edge]
        s_ref[:, 0:ctx_len] = s_c
        s_ref[:, ctx_len:ctx_len + span] = s_w
        m = jnp.maximum(jnp.maximum(jnp.max(s_c, axis=-1, keepdims=True),
                                    jnp.max(s_w, axis=-1, keepdims=True)), sink)
        s_ref[:, ctx_len + span:] = jnp.broadcast_to(m, (WIN_G * tq, V7X_LANES))

    def attend(t, s_ref):
        m = s_ref[:, ctx_len + span:ctx_len + span + 1]
        e_c = jnp.exp(s_ref[:, 0:ctx_len] - m)
        e_w = jnp.exp(s_ref[:, ctx_len:ctx_len + span] - m)
        denom = (jnp.sum(e_c, axis=-1, keepdims=True) + jnp.sum(e_w, axis=-1, keepdims=True)
                 + jnp.exp(sink - m))
        o4 = (_dot(e_c.astype(BF16), vcr_ref[...])
              + _dot(e_w.astype(BF16), vr_ref[pl.ds(window_start(t), span), :])) / denom
        out = jnp.zeros((tq, WIN_GW), F32)
        for g in range(WIN_G):
            out = out + jnp.where(lane_group == g, o4[g * tq:(g + 1) * tq], 0.0)
        o_ref[0, pl.ds(pl.multiple_of(t * tq, tq), tq), :] = out.astype(BF16)

    def quad(i, carry):
        t = 4 * i
        scores(t + 2, s_refs[2])
        scores(t + 3, s_refs[3])
        attend(t, s_refs[0])
        attend(t + 1, s_refs[1])
        scores(t + 4, s_refs[0])
        scores(t + 5, s_refs[1])
        attend(t + 2, s_refs[2])
        attend(t + 3, s_refs[3])
        return carry

    scores(0, s_refs[0])
    scores(1, s_refs[1])
    lax.fori_loop(0, nq // 4, quad, 0)


def _win_attn(wq, wk, wv, wkc, wvc, rep, bias, sink):
    nbatch, seq, _ = wq.shape
    ctx_len = wkc.shape[1]
    assert (seq // Q_BLOCK) % 4 == 0
    span = Q_BLOCK + 2 * WINDOW
    srows, scols = WIN_G * Q_BLOCK, ctx_len + span + V7X_LANES
    per_b = lambda b, j: (b, 0, 0)
    per_bh = lambda b, j: (b, 0, j)
    return pl.pallas_call(
        _win_attn_kernel,
        out_shape=jax.ShapeDtypeStruct((nbatch, seq, WIN_Q), BF16),
        grid=(nbatch, H_WIN_KV),
        in_specs=[pl.BlockSpec((1, seq, WIN_GW), per_bh),
                  pl.BlockSpec((1, seq, WIN_KV), per_b),
                  pl.BlockSpec((1, seq, WIN_KV), per_b),
                  pl.BlockSpec((1, ctx_len, WIN_KV), per_b),
                  pl.BlockSpec((1, ctx_len, WIN_KV), per_b),
                  pl.BlockSpec((1, WIN_KV, WIN_GW), lambda b, j: (j, 0, 0)),
                  _const_spec(bias.shape),
                  pl.BlockSpec(memory_space=pltpu.SMEM)],
        out_specs=pl.BlockSpec((1, seq, WIN_GW), per_bh),
        scratch_shapes=[pltpu.VMEM((seq, WIN_GW), BF16), pltpu.VMEM((seq, WIN_GW), BF16),
                        pltpu.VMEM((ctx_len, WIN_GW), BF16), pltpu.VMEM((ctx_len, WIN_GW), BF16),
                        *[pltpu.VMEM((srows, scols), F32) for _ in range(4)]],
        compiler_params=_cparams("arbitrary", "arbitrary"),
        name="window_attention",
    )(wq, wk, wv, wkc, wvc, rep, bias, sink)


def _cos_sin(num, den):
    ang = 2.0 * np.pi * (np.asarray(num, np.int64) % den).astype(np.float64) / den
    return np.cos(ang), np.sin(ang)


def _channel_dft_table():
    k = np.arange(FOURIER_GW)
    c, s = _cos_sin(np.outer(k, k), FOURIER_GW)
    return np.concatenate([c, -s], axis=1)


def _stage_a_table(n1c):
    k = np.arange(n1c)
    c, s = _cos_sin(np.outer(k, k), n1c)
    return np.kron(np.block([[c, s], [-s, c]]), np.eye(FFT_RB))


def _stage_b_table(norm):
    k = np.arange(V7X_LANES)
    c, s = _cos_sin(np.outer(k, k), V7X_LANES)
    kb = np.zeros((V7X_LANES, FFT_RB, 2, FFT_RB, V7X_LANES))
    for i in range(FFT_RB):
        kb[:, i, 0, i, :] = c / norm
        kb[:, i, 1, i, :] = s / norm
    return kb.reshape(V7X_LANES * FFT_RB, 2 * FFT_RB * V7X_LANES)


def _twiddle_tables(seq):
    n1c = seq // V7X_LANES
    c, s = _cos_sin(np.outer(np.arange(n1c), np.arange(V7X_LANES)), seq)
    expand = lambda t: np.repeat(t[:, :, None], V7X_LANES, axis=2)
    return expand(c), expand(s)


def _real_dft_table(n, norm):
    k = np.arange(n)
    c, s = _cos_sin(np.outer(k, k), n)
    return np.concatenate([c, s], axis=1) / norm


def _rope_tables(seq):
    t = np.arange(seq)
    axis_dim = HEAD_DIM // 2
    inv_freq = ROPE_BASE ** (-np.arange(0, axis_dim, 2, dtype=np.float64) / axis_dim)
    ang_r = (t // GRID_W).astype(np.float64)[:, None] * inv_freq[None, :]
    ang_c = (t % GRID_W).astype(np.float64)[:, None] * inv_freq[None, :]
    cos = np.concatenate([np.cos(ang_r)] * 2 + [np.cos(ang_c)] * 2, axis=1)
    sin = np.concatenate([-np.sin(ang_r), np.sin(ang_r), -np.sin(ang_c), np.sin(ang_c)], axis=1)
    reps = V7X_LANES // HEAD_DIM
    return np.tile(cos, (1, reps)), np.tile(sin, (1, reps))


def _window_bias_table():
    span = Q_BLOCK + 2 * WINDOW
    q = np.arange(Q_BLOCK)[:, None]
    k = np.arange(span)[None, :]
    out = []
    for start in (0, WINDOW, 2 * WINDOW):
        vis = np.abs(k - start - q) <= WINDOW
        out.append(np.tile(np.where(vis, 0.0, NEG_INF), (WIN_G, 1)))
    return np.stack(out).astype(np.float32)


def _replication_table():
    rep = np.zeros((H_WIN_KV, WIN_KV, WIN_GW), np.float32)
    for j in range(H_WIN_KV):
        for g in range(WIN_G):
            for d in range(HEAD_DIM):
                rep[j, j * HEAD_DIM + d, g * HEAD_DIM + d] = 1.0
    return rep


def _fourier_conv_mix(h2d, mod, gpre, gpost, w_in, conv_w, conv_b, ln_g, ln_b, w_out, nbatch, tm):
    t = h2d.shape[0]
    seq = t // nbatch
    table = lambda a: jnp.asarray(a, F32).astype(BF16)
    cw = table(_channel_dft_table())
    norm = math.sqrt(seq * FOURIER_GW)
    if seq % (FFT_RB * V7X_LANES) == 0:
        n1c = seq // V7X_LANES
        twc, tws = _twiddle_tables(seq)
        z, u = _ev_fft_a(h2d, mod, gpre, w_in, cw, table(_stage_a_table(n1c)),
                         jnp.asarray(twc, F32), jnp.asarray(tws, F32), nbatch)
        fa = _ev_fft_b(z, table(_stage_b_table(norm)))
        u = u.reshape(t, D_CONV)
    else:
        p, u = _ev_proj(h2d, mod, gpre, w_in, cw, nbatch, tm)
        fa = _dft_real(p.reshape(nbatch, 2 * seq, D_FOURIER), table(_real_dft_table(seq, norm)),
                       D_FOURIER)
    fa = fa.reshape(t, D_FOURIER)
    uc = _conv_ln_swish(u, conv_w, conv_b, ln_g, ln_b, nbatch, tm)
    out_tm = math.gcd(seq, OUT_TM)
    return _outproj(fa, uc, h2d, mod, gpost, w_out, out_tm, seq // out_tm)


def _diff_window_mix(x2d, c2d, modx, modc, gpre, gpost, w_in, lam_vec, subln_g, sink, w_out,
                     lam_init, nbatch):
    t = x2d.shape[0]
    seq = t // nbatch
    ctx_len = c2d.shape[0] // nbatch
    cos, sin = _rope_tables(seq)
    dq, dk, vt, wq, wk, wv = _od_proj(x2d, modx, gpre, w_in, jnp.asarray(cos, F32),
                                      jnp.asarray(sin, F32), nbatch, KV_CHUNK)
    dkc, vtc, wkc, wvc = _od_proj_ctx(c2d, modc, gpre, w_in, nbatch)
    b3 = lambda a, n: a.reshape(nbatch, n, a.shape[-1])
    diff = _diff_attn(b3(dq, seq), b3(dk, seq), vt, b3(dkc, ctx_len), vtc, lam_vec, subln_g, lam_init)
    win = _win_attn(b3(wq, seq), b3(wk, seq), b3(wv, seq), b3(wkc, ctx_len), b3(wvc, ctx_len),
                    jnp.asarray(_replication_table(), F32).astype(BF16),
                    jnp.asarray(_window_bias_table(), F32), sink)
    out_tm = math.gcd(seq, OUT_TM)
    return _outproj(diff.reshape(t, DIFF_V), win.reshape(t, WIN_Q), x2d, modx, gpost, w_out,
                    out_tm, seq // out_tm)


def kernel(x, c, ctx, c_ctx, w_mod, b_mod, norm_pre, norm_post, ffn_w_in, ffn_w_out, ev_w_in, ev_conv_w, ev_conv_b, ev_ln_g, ev_ln_b, ev_w_out, od_w_in, od_lambda, od_subln_g, od_sink, od_w_out):
    nbatch, seq, d = x.shape
    ctx_len = ctx.shape[1]
    depth = w_mod.shape[0]
    assert d == D_MODEL and depth == DEPTH == 2 and ctx_len == KV_CHUNK
    assert seq % FFN_TM == 0 and seq % (8 * V7X_LANES) == 0

    mod_rows = 8
    cs = jnp.concatenate([c, c_ctx[None, :], jnp.zeros((mod_rows - nbatch - 1, d), F32)], axis=0)
    mod = _modulation(cs, w_mod, b_mod).reshape(depth, mod_rows, N_MOD, d)
    row = lambda v: v.reshape(1, -1)

    x2d = x.reshape(nbatch * seq, d)
    c2d = ctx.reshape(nbatch * ctx_len, d)
    ffn_in = ffn_w_in.astype(BF16)
    ffn_out = ffn_w_out.astype(BF16)

    def ffn_x(x2d, l, s, k):
        return _ffn(x2d, mod[l, :nbatch], row(norm_pre[l, s]), row(norm_post[l, s]),
                    ffn_in, ffn_out, (l, k), s, FFN_TM, seq // FFN_TM)

    def ffn_c(c2d, l, s, k):
        return _ffn(c2d, mod[l, nbatch:nbatch + 1], row(norm_pre[l, s]), row(norm_post[l, s]),
                    ffn_in, ffn_out, (l, k), s, ctx_len, 1)

    l = 0
    x2d = ffn_x(x2d, l, 0, 0)
    c2d = ffn_c(c2d, l, 0, 0)
    ev_args = (row(norm_pre[l, 1]), row(norm_post[l, 1]), ev_w_in[0].astype(BF16), ev_conv_w[0],
               row(ev_conv_b[0]), row(ev_ln_g[0]), row(ev_ln_b[0]), ev_w_out[0].astype(BF16))
    x2d = _fourier_conv_mix(x2d, mod[l, :nbatch], *ev_args, nbatch, ROW_TM)
    c2d = _fourier_conv_mix(c2d, mod[l, nbatch:nbatch + 1], *ev_args, nbatch, ctx_len)
    x2d = ffn_x(x2d, l, 2, 1)
    c2d = ffn_c(c2d, l, 2, 1)

    l = 1
    x2d = ffn_x(x2d, l, 0, 0)
    c2d = ffn_c(c2d, l, 0, 0)
    lam_init = 0.8 - 0.6 * math.exp(-0.3 * l)
    x2d = _diff_window_mix(x2d, c2d, mod[l, :nbatch], mod[l, nbatch:nbatch + 1],
                           row(norm_pre[l, 1]), row(norm_post[l, 1]), od_w_in[0].astype(BF16),
                           od_lambda[0], row(od_subln_g[0]), od_sink[0], od_w_out[0].astype(BF16),
                           lam_init, nbatch)
    x2d = ffn_x(x2d, l, 2, 1)
    return x2d.reshape(nbatch, seq, d)
```

```python
import functools
import math

import numpy as np
import jax
import jax.numpy as jnp
from jax import lax
from jax.experimental import pallas as pl
from jax.experimental.pallas import tpu as pltpu

F32 = jnp.float32
BF16 = jnp.bfloat16

D_MODEL = 1024
DEPTH = 2
GRID_W = 64
N_MOD = 9
EPS = 1e-6
NEG_INF = -1e30
HALF_STEP = 0.5
D_FF = 2816
FOURIER_GROUPS = 4
FOURIER_GW = D_MODEL // 8
D_FOURIER = FOURIER_GROUPS * FOURIER_GW
D_CONV = D_MODEL // 2
CONV_W = 31
HEAD_DIM = 64
ROPE_BASE = 10000.0
H_DIFF = 4
DIFF_QK = H_DIFF * 2 * HEAD_DIM
DIFF_V = H_DIFF * 2 * HEAD_DIM
H_WIN = 8
H_WIN_KV = 2
WIN_G = H_WIN // H_WIN_KV
WIN_Q = H_WIN * HEAD_DIM
WIN_KV = H_WIN_KV * HEAD_DIM
WINDOW = 128
Q_BLOCK = 128

V7X_LANES = 128
V7X_MXU_DIM = 256
V7X_BF16_SUBLANES = 16
V7X_VMEM_LIMIT_BYTES = 56 * 1024 * 1024

FFN_TM = 1024
FFN_SUB = 256
FFN_FC = V7X_MXU_DIM
ROW_TM = 256
LOG2_E = math.log2(math.e)
KV_CHUNK = 256
FFT_RB = 8
CONV_RB = 32
CONV_HALO = V7X_BF16_SUBLANES
CONV_BLK = 128
CONV_SPAN = CONV_BLK + 2 * CONV_HALO
CONV_K = V7X_MXU_DIM


def _cparams(*sem):
    return pltpu.CompilerParams(dimension_semantics=sem, vmem_limit_bytes=V7X_VMEM_LIMIT_BYTES)


def _const_spec(shape):
    n = len(shape)
    return pl.BlockSpec(shape, lambda *_: (0,) * n, pipeline_mode=pl.Buffered(1))


def _dot(a, b):
    return jnp.dot(a, b, preferred_element_type=F32)


def _dot_nt(a, b):
    return lax.dot_general(a, b, (((1,), (1,)), ((), ())), preferred_element_type=F32)


def _rms(xf, g):
    return xf * lax.rsqrt(jnp.mean(xf * xf, axis=-1, keepdims=True) + EPS) * g


def _mrow(m_ref, j):
    return m_ref[0, j:j + 1, :]


def _pre(x, g, m_ref, s):
    return _rms(x, g) * (1.0 + _mrow(m_ref, 3 * s + 1)) + _mrow(m_ref, 3 * s)


def _mod_spec(mod, tiles_per_batch):
    d = mod.shape[-1]
    if mod.shape[0] == 1:
        return pl.BlockSpec((1, N_MOD, d), lambda i: (0, 0, 0))
    return pl.BlockSpec((1, N_MOD, d), lambda i: (i // tiles_per_batch, 0, 0))


def _sigmoid(x):
    return 1.0 / (1.0 + jnp.exp(-x))


def _mod_kernel(c_ref, w_ref, b_ref, o_ref):
    c = c_ref[...]
    h = c * _sigmoid(c)
    o_ref[0] = jnp.dot(h, w_ref[0], preferred_element_type=F32,
                       precision=lax.Precision.HIGHEST) + b_ref[0]


def _modulation(cs, w_mod, b_mod):
    depth, d, nd = w_mod.shape
    rows = cs.shape[0]
    return pl.pallas_call(
        _mod_kernel,
        out_shape=jax.ShapeDtypeStruct((depth, rows, nd), F32),
        grid=(depth, nd // d),
        in_specs=[pl.BlockSpec((rows, d), lambda l, j: (0, 0)),
                  pl.BlockSpec((1, d, d), lambda l, j: (l, 0, j)),
                  pl.BlockSpec((1, 1, d), lambda l, j: (l, 0, j))],
        out_specs=pl.BlockSpec((1, rows, d), lambda l, j: (l, 0, j)),
        compiler_params=_cparams("arbitrary", "arbitrary"),
        name="modulation",
    )(cs, w_mod, b_mod.reshape(depth, 1, nd))


def _swiglu_rows(s, x, rows, m_ref, gpre_ref, gpost_ref, win_ref, wout_ref, o_ref, a_ref):
    h = _pre(x, gpre_ref[...], m_ref, s).astype(BF16)
    for c in range(D_FF // FFN_FC):
        lo = c * FFN_FC
        g = _dot(h, win_ref[:, lo:lo + FFN_FC])
        u = _dot(h, win_ref[:, D_FF + lo:D_FF + lo + FFN_FC])
        a_ref[rows, lo:lo + FFN_FC] = (g * _sigmoid(g) * u).astype(BF16)
    y = _dot(a_ref[rows, :], wout_ref[...])
    o_ref[rows, :] = x + HALF_STEP * _mrow(m_ref, 3 * s + 2) * _rms(y, gpost_ref[...])


def _row_subtiles(n):
    sub = min(FFN_SUB, n)
    return [slice(j * sub, (j + 1) * sub) for j in range(n // sub)]


def _ffn_kernel(s, x_ref, m_ref, gpre_ref, gpost_ref, win_ref, wout_ref, o_ref, a_ref):
    for rows in _row_subtiles(x_ref.shape[0]):
        _swiglu_rows(s, x_ref[rows, :], rows, m_ref, gpre_ref, gpost_ref, win_ref, wout_ref, o_ref, a_ref)


def _mix_ffn_kernel(s, ma_ref, mb_ref, wmix_ref, gmix_ref, x_ref, m_ref, gpre_ref, gpost_ref,
                    win_ref, wout_ref, o_ref, a_ref):
    ka = ma_ref.shape[1]
    for rows in _row_subtiles(x_ref.shape[0]):
        y = (_dot(ma_ref[rows, :].astype(BF16), wmix_ref[:ka])
             + _dot(mb_ref[rows, :].astype(BF16), wmix_ref[ka:]))
        x = x_ref[rows, :] + _mrow(m_ref, 5) * _rms(y, gmix_ref[...])
        _swiglu_rows(s, x, rows, m_ref, gpre_ref, gpost_ref, win_ref, wout_ref, o_ref, a_ref)


def _stacked_weight_spec(w, lk):
    l, k = lk
    return pl.BlockSpec((None, None) + w.shape[2:], lambda i: (l, k, 0, 0),
                        pipeline_mode=pl.Buffered(1))


def _ffn(x2d, mod, gpre, gpost, w_in, w_out, lk, s, tm, tiles_per_batch):
    t, d = x2d.shape
    return pl.pallas_call(
        functools.partial(_ffn_kernel, s),
        out_shape=jax.ShapeDtypeStruct((t, d), F32),
        grid=(t // tm,),
        in_specs=[pl.BlockSpec((tm, d), lambda i: (i, 0)),
                  _mod_spec(mod, tiles_per_batch),
                  _const_spec((1, d)), _const_spec((1, d)),
                  _stacked_weight_spec(w_in, lk), _stacked_weight_spec(w_out, lk)],
        out_specs=pl.BlockSpec((tm, d), lambda i: (i, 0)),
        scratch_shapes=[pltpu.VMEM((tm, D_FF), BF16)],
        compiler_params=_cparams("arbitrary"),
        name="swiglu_halfstep",
    )(x2d, mod, gpre, gpost, w_in, w_out)


def _mix_ffn(ma, mb, w_mix, g_mix, x2d, mod, gpre, gpost, w_in, w_out, lk, s, tm, tiles_per_batch):
    t, d = x2d.shape
    row = lambda i: (i, 0)
    return pl.pallas_call(
        functools.partial(_mix_ffn_kernel, s),
        out_shape=jax.ShapeDtypeStruct((t, d), F32),
        grid=(t // tm,),
        in_specs=[pl.BlockSpec((tm, ma.shape[1]), row), pl.BlockSpec((tm, mb.shape[1]), row),
                  _const_spec(w_mix.shape), _const_spec((1, d)),
                  pl.BlockSpec((tm, d), row),
                  _mod_spec(mod, tiles_per_batch),
                  _const_spec((1, d)), _const_spec((1, d)),
                  _stacked_weight_spec(w_in, lk), _stacked_weight_spec(w_out, lk)],
        out_specs=pl.BlockSpec((tm, d), row),
        scratch_shapes=[pltpu.VMEM((tm, D_FF), BF16)],
        compiler_params=_cparams("arbitrary"),
        name="outproj_swiglu_halfstep",
    )(ma, mb, w_mix, g_mix, x2d, mod, gpre, gpost, w_in, w_out)


def _ev_proj_kernel(x_ref, m_ref, gpre_ref, w_ref, cw_ref, p_ref, u_ref):
    h = _pre(x_ref[...], gpre_ref[...], m_ref, 1).astype(BF16)
    a = _dot(h, w_ref[:, :D_FOURIER])
    g1 = _dot(h, w_ref[:, D_FOURIER:D_FOURIER + D_CONV])
    g2 = _dot(h, w_ref[:, D_FOURIER + D_CONV:])
    u_ref[...] = (g1 * _sigmoid(g2)).astype(BF16)
    gw = FOURIER_GW
    for g in range(FOURIER_GROUPS):
        pg = _dot(a[:, g * gw:(g + 1) * gw].astype(BF16), cw_ref[...])
        p_ref[0, 0, :, g * gw:(g + 1) * gw] = pg[:, :gw].astype(BF16)
        p_ref[0, 1, :, g * gw:(g + 1) * gw] = pg[:, gw:].astype(BF16)


def _ev_proj(x2d, mod, gpre, w_in, cw, nbatch, tm):
    t, d = x2d.shape
    seq = t // nbatch
    tpb = seq // tm
    return pl.pallas_call(
        _ev_proj_kernel,
        out_shape=(jax.ShapeDtypeStruct((nbatch, 2, seq, D_FOURIER), BF16),
                   jax.ShapeDtypeStruct((t, D_CONV), BF16)),
        grid=(t // tm,),
        in_specs=[pl.BlockSpec((tm, d), lambda i: (i, 0)),
                  _mod_spec(mod, tpb),
                  _const_spec((1, d)), _const_spec(w_in.shape), _const_spec(cw.shape)],
        out_specs=(pl.BlockSpec((1, 2, tm, D_FOURIER), lambda i: (i // tpb, 0, i % tpb, 0)),
                   pl.BlockSpec((tm, D_CONV), lambda i: (i, 0))),
        compiler_params=_cparams("arbitrary"),
        name="even_proj",
    )(x2d, mod, gpre, w_in, cw)


def _ev_fft_a_kernel(x_ref, m_ref, gpre_ref, w_ref, cw_ref, ka_ref, twc_ref, tws_ref, z_ref, u_ref):
    n1c = x_ref.shape[1]
    rows = n1c * FFT_RB
    x = x_ref[0].reshape(rows, D_MODEL)
    h = _pre(x, gpre_ref[...], m_ref, 1).astype(BF16)
    a = _dot(h, w_ref[:, :D_FOURIER])
    g1 = _dot(h, w_ref[:, D_FOURIER:D_FOURIER + D_CONV])
    g2 = _dot(h, w_ref[:, D_FOURIER + D_CONV:])
    u_ref[0] = (g1 * _sigmoid(g2)).reshape(n1c, FFT_RB, D_CONV)
    gw = FOURIER_GW
    pr, pi = [], []
    for g in range(FOURIER_GROUPS):
        pg = _dot(a[:, g * gw:(g + 1) * gw].astype(BF16), cw_ref[...])
        pr.append(pg[:, :gw])
        pi.append(pg[:, gw:])
    p = jnp.concatenate([jnp.concatenate(pr, axis=1), jnp.concatenate(pi, axis=1)], axis=0)
    z = _dot(ka_ref[...], p.astype(BF16))
    zr, zi = z[:rows], z[rows:]
    tc = jnp.concatenate([twc_ref[...].reshape(rows, V7X_LANES)] * FOURIER_GROUPS, axis=1)
    ts = jnp.concatenate([tws_ref[...].reshape(rows, V7X_LANES)] * FOURIER_GROUPS, axis=1)
    z_ref[0, 0] = (zr * tc + zi * ts).reshape(n1c, FFT_RB, D_FOURIER)
    z_ref[0, 1] = (zi * tc - zr * ts).reshape(n1c, FFT_RB, D_FOURIER)


def _ev_fft_a(x2d, mod, gpre, w_in, cw, ka, twc, tws, nbatch):
    t, d = x2d.shape
    seq = t // nbatch
    n1c = seq // V7X_LANES
    nrb = V7X_LANES // FFT_RB
    x4 = x2d.reshape(nbatch, n1c, V7X_LANES, d)
    blk = lambda w: pl.BlockSpec((1, n1c, FFT_RB, w), lambda i: (i // nrb, 0, i % nrb, 0))
    return pl.pallas_call(
        _ev_fft_a_kernel,
        out_shape=(jax.ShapeDtypeStruct((nbatch, 2, n1c, V7X_LANES, D_FOURIER), F32),
                   jax.ShapeDtypeStruct((nbatch, n1c, V7X_LANES, D_CONV), F32)),
        grid=(nbatch * nrb,),
        in_specs=[blk(d), _mod_spec(mod, nrb), _const_spec((1, d)), _const_spec(w_in.shape),
                  _const_spec(cw.shape), _const_spec(ka.shape),
                  pl.BlockSpec((n1c, FFT_RB, V7X_LANES), lambda i: (0, i % nrb, 0)),
                  pl.BlockSpec((n1c, FFT_RB, V7X_LANES), lambda i: (0, i % nrb, 0))],
        out_specs=(pl.BlockSpec((1, 2, n1c, FFT_RB, D_FOURIER), lambda i: (i // nrb, 0, 0, i % nrb, 0)),
                   blk(D_CONV)),
        compiler_params=_cparams("arbitrary"),
        name="even_proj_fft_a",
    )(x4, mod, gpre, w_in, cw, ka, twc, tws)


def _ev_fft_b_kernel(kb_ref, z_ref, o_ref):
    z = z_ref[0].reshape(2 * FFT_RB * V7X_LANES, D_FOURIER).astype(BF16)
    o_ref[0] = _dot(kb_ref[...], z).reshape(V7X_LANES, FFT_RB, D_FOURIER)


def _ev_fft_b(z, kb):
    nbatch, _, n1c, _, nc = z.shape
    return pl.pallas_call(
        _ev_fft_b_kernel,
        out_shape=jax.ShapeDtypeStruct((nbatch, V7X_LANES, n1c, nc), F32),
        grid=(nbatch, n1c // FFT_RB),
        in_specs=[_const_spec(kb.shape),
                  pl.BlockSpec((1, 2, FFT_RB, V7X_LANES, nc), lambda b, j: (b, 0, j, 0, 0))],
        out_specs=pl.BlockSpec((1, V7X_LANES, FFT_RB, nc), lambda b, j: (b, 0, j, 0)),
        compiler_params=_cparams("arbitrary", "arbitrary"),
        name="fft_b",
    )(kb, z)


def _dft_real_kernel(mat_ref, z_ref, o_ref):
    o_ref[0] = _dot(mat_ref[...], z_ref[0]).astype(o_ref.dtype)


def _dft_real(z, mat, nb):
    nbatch, k, n = z.shape
    m = mat.shape[0]
    return pl.pallas_call(
        _dft_real_kernel,
        out_shape=jax.ShapeDtypeStruct((nbatch, m, n), BF16),
        grid=(nbatch, n // nb),
        in_specs=[_const_spec(mat.shape), pl.BlockSpec((1, k, nb), lambda b, j: (b, 0, j))],
        out_specs=pl.BlockSpec((1, m, nb), lambda b, j: (b, 0, j)),
        compiler_params=_cparams("arbitrary", "arbitrary"),
        name="dft_real",
    )(mat, z)


def _conv_kernel(tpb, u_ref, up_ref, un_ref, sh_ref, cw_ref, cb_ref, lg_ref, lb_ref, o_ref,
                 ext_ref, cp_ref):
    tm = u_ref.shape[0]
    j = pl.program_id(0) % tpb
    halo = CONV_HALO
    ext_ref[0:halo] = jnp.where(j > 0, up_ref[...], 0).astype(BF16)
    ext_ref[halo:halo + tm] = u_ref[...].astype(BF16)
    ext_ref[halo + tm:tm + 2 * halo] = jnp.where(j < tpb - 1, un_ref[...], 0).astype(BF16)
    ext_ref[tm + 2 * halo:] = jnp.zeros((CONV_K - CONV_SPAN, D_CONV), BF16)
    cb, lg, lb = cb_ref[...], lg_ref[...], lb_ref[...]
    for blk in range(tm // CONV_BLK):
        b0 = blk * CONV_BLK
        cp_ref[blk] = _dot(sh_ref[...], ext_ref[b0:b0 + CONV_K, :])
        for c in range(CONV_BLK // CONV_RB):
            base = c * CONV_RB
            acc = None
            for t in range(CONV_W):
                r, off = t % 8, t - t % 8
                rows = cp_ref[blk, r * CONV_SPAN + base + off:r * CONV_SPAN + base + off + CONV_RB, :]
                term = cw_ref[t:t + 1, :] * rows
                acc = term if acc is None else acc + term
            v = acc + cb
            vc = v - jnp.mean(v, axis=-1, keepdims=True)
            var = jnp.mean(vc * vc, axis=-1, keepdims=True)
            y = vc * lax.rsqrt(var + EPS) * lg + lb
            o_ref[b0 + base:b0 + base + CONV_RB, :] = (y * _sigmoid(y)).astype(BF16)


def _conv_shift_table():
    first = CONV_HALO - CONV_W // 2
    sh = np.zeros((8, CONV_SPAN, CONV_K), np.float32)
    for r in range(8):
        for m in range(CONV_SPAN - first - r):
            sh[r, m, m + first + r] = 1.0
    return sh.reshape(8 * CONV_SPAN, CONV_K)


def _conv_ln_swish(u, conv_w, conv_b, ln_g, ln_b, nbatch, tm):
    t, dc = u.shape
    tpb = (t // nbatch) // tm
    hb = tm // CONV_HALO
    last = t // CONV_HALO - 1
    shift = jnp.asarray(_conv_shift_table(), F32).astype(BF16)
    return pl.pallas_call(
        functools.partial(_conv_kernel, tpb),
        out_shape=jax.ShapeDtypeStruct((t, dc), BF16),
        grid=(t // tm,),
        in_specs=[pl.BlockSpec((tm, dc), lambda i: (i, 0)),
                  pl.BlockSpec((CONV_HALO, dc), lambda i: (jnp.maximum(i * hb - 1, 0), 0)),
                  pl.BlockSpec((CONV_HALO, dc), lambda i: (jnp.minimum((i + 1) * hb, last), 0)),
                  _const_spec(shift.shape),
                  _const_spec(conv_w.shape), _const_spec((1, dc)), _const_spec((1, dc)),
                  _const_spec((1, dc))],
        out_specs=pl.BlockSpec((tm, dc), lambda i: (i, 0)),
        scratch_shapes=[pltpu.VMEM((tm + CONV_K - CONV_BLK, dc), BF16),
                        pltpu.VMEM((tm // CONV_BLK, 8 * CONV_SPAN, dc), F32)],
        compiler_params=_cparams("arbitrary"),
        name="conv_ln_swish",
    )(u, u, u, shift, conv_w, conv_b, ln_g, ln_b)


def _rope(z, cos, sin, even_block):
    out = []
    for j in range(z.shape[1] // V7X_LANES):
        zs = z[:, j * V7X_LANES:(j + 1) * V7X_LANES]
        sw = jnp.where(even_block, pltpu.roll(zs, V7X_LANES - 16, 1), pltpu.roll(zs, 16, 1))
        out.append(zs * cos + sw * sin)
    return out[0] if len(out) == 1 else jnp.concatenate(out, axis=1)


def _od_proj_kernel(x_ref, m_ref, gpre_ref, w_ref, cos_ref, sin_ref,
                    dq_ref, dk_ref, vt_ref, wq_ref, wk_ref, wv_ref):
    h = _pre(x_ref[...], gpre_ref[...], m_ref, 1).astype(BF16)
    cos, sin = cos_ref[...], sin_ref[...]
    lane = lax.broadcasted_iota(jnp.int32, (1, V7X_LANES), 1)
    even_block = ((lane >> 4) & 1) == 0
    scale = HEAD_DIM ** -0.5
    o = 0
    dq_ref[...] = (_rope(_dot(h, w_ref[:, o:o + DIFF_QK]), cos, sin, even_block)
                   * (scale * LOG2_E)).astype(BF16)
    o += DIFF_QK
    dk_ref[...] = _rope(_dot(h, w_ref[:, o:o + DIFF_QK]), cos, sin, even_block).astype(BF16)
    o += DIFF_QK
    vt_ref[0] = _dot(h, w_ref[:, o:o + DIFF_V]).T.astype(BF16)
    o += DIFF_V
    wq_ref[...] = (_rope(_dot(h, w_ref[:, o:o + WIN_Q]), cos, sin, even_block) * scale).astype(BF16)
    o += WIN_Q
    wk_ref[...] = _rope(_dot(h, w_ref[:, o:o + WIN_KV]), cos, sin, even_block).astype(BF16)
    o += WIN_KV
    wv_ref[...] = _dot(h, w_ref[:, o:o + WIN_KV]).astype(BF16)


def _od_proj(x2d, mod, gpre, w_in, cos, sin, nbatch, tm):
    t, d = x2d.shape
    seq = t // nbatch
    tpb = seq // tm
    row = lambda i: (i, 0)
    return pl.pallas_call(
        _od_proj_kernel,
        out_shape=(jax.ShapeDtypeStruct((t, DIFF_QK), BF16),
                   jax.ShapeDtypeStruct((t, DIFF_QK), BF16),
                   jax.ShapeDtypeStruct((nbatch, DIFF_V, seq), BF16),
                   jax.ShapeDtypeStruct((t, WIN_Q), BF16),
                   jax.ShapeDtypeStruct((t, WIN_KV), BF16),
                   jax.ShapeDtypeStruct((t, WIN_KV), BF16)),
        grid=(t // tm,),
        in_specs=[pl.BlockSpec((tm, d), row),
                  _mod_spec(mod, tpb),
                  _const_spec((1, d)), _const_spec(w_in.shape),
                  pl.BlockSpec((tm, V7X_LANES), lambda i: (i % tpb, 0)),
                  pl.BlockSpec((tm, V7X_LANES), lambda i: (i % tpb, 0))],
        out_specs=(pl.BlockSpec((tm, DIFF_QK), row),
                   pl.BlockSpec((tm, DIFF_QK), row),
                   pl.BlockSpec((1, DIFF_V, tm), lambda i: (i // tpb, 0, i % tpb)),
                   pl.BlockSpec((tm, WIN_Q), row),
                   pl.BlockSpec((tm, WIN_KV), row),
                   pl.BlockSpec((tm, WIN_KV), row)),
        compiler_params=_cparams("arbitrary"),
        name="odd_proj",
    )(x2d, mod, gpre, w_in, cos, sin)


def _od_proj_ctx_kernel(x_ref, m_ref, gpre_ref, w_ref, dk_ref, vt_ref, wk_ref, wv_ref):
    h = _pre(x_ref[...], gpre_ref[...], m_ref, 1).astype(BF16)
    o = DIFF_QK
    dk_ref[...] = _dot(h, w_ref[:, o:o + DIFF_QK]).astype(BF16)
    o += DIFF_QK
    vt_ref[0] = _dot(h, w_ref[:, o:o + DIFF_V]).T.astype(BF16)
    o += DIFF_V + WIN_Q
    wk_ref[...] = _dot(h, w_ref[:, o:o + WIN_KV]).astype(BF16)
    o += WIN_KV
    wv_ref[...] = _dot(h, w_ref[:, o:o + WIN_KV]).astype(BF16)


def _od_proj_ctx(c2d, mod, gpre, w_in, nbatch):
    t, d = c2d.shape
    tm = t // nbatch
    row = lambda i: (i, 0)
    return pl.pallas_call(
        _od_proj_ctx_kernel,
        out_shape=(jax.ShapeDtypeStruct((t, DIFF_QK), BF16),
                   jax.ShapeDtypeStruct((nbatch, DIFF_V, tm), BF16),
                   jax.ShapeDtypeStruct((t, WIN_KV), BF16),
                   jax.ShapeDtypeStruct((t, WIN_KV), BF16)),
        grid=(nbatch,),
        in_specs=[pl.BlockSpec((tm, d), row), _const_spec(mod.shape),
                  _const_spec((1, d)), _const_spec(w_in.shape)],
        out_specs=(pl.BlockSpec((tm, DIFF_QK), row),
                   pl.BlockSpec((1, DIFF_V, tm), lambda i: (i, 0, 0)),
                   pl.BlockSpec((tm, WIN_KV), row),
                   pl.BlockSpec((tm, WIN_KV), row)),
        compiler_params=_cparams("arbitrary"),
        name="odd_proj_ctx",
    )(c2d, mod, gpre, w_in)


def _diff_attn_kernel(lam_init, q_ref, kx_ref, kc_ref, vx_ref, vc_ref, lam_ref, g_ref, o_ref,
                      k_all, v_all, sa_ref, sb_ref, sc_ref):
    tq = Q_BLOCK
    seq = q_ref.shape[1]
    ctx_len = kc_ref.shape[1]
    nkeys = ctx_len + seq
    nq = seq // tq
    hd2 = 2 * HEAD_DIM
    k_all[0:ctx_len] = kc_ref[0]
    k_all[ctx_len:nkeys] = kx_ref[0]
    v_all[0:hd2, 0:ctx_len] = vc_ref[0]
    v_all[0:hd2, ctx_len:nkeys] = vx_ref[0]
    v_all[hd2:, :] = jnp.ones((V7X_BF16_SUBLANES, nkeys), BF16)
    lv = lam_ref[...]
    lam = (jnp.exp(jnp.sum(lv[0:1] * lv[1:2], axis=-1, keepdims=True))
           - jnp.exp(jnp.sum(lv[2:3] * lv[3:4], axis=-1, keepdims=True)) + lam_init)
    lane = lax.broadcasted_iota(jnp.int32, (1, hd2), 1)
    first_half = lane < HEAD_DIM
    zero = jnp.zeros((), BF16)

    def scores(t, s_ref):
        r0 = pl.multiple_of(t * tq, tq)
        q = q_ref[0, pl.ds(r0, tq), :]
        qq = jnp.concatenate([jnp.where(first_half, q, zero),
                              jnp.where(first_half, zero, q)], axis=0)
        s = _dot_nt(k_all[...], qq)
        s_ref[...] = s
        return jnp.max(s, axis=0, keepdims=True)

    def attend(t, s_ref, mcol):
        p = jnp.exp2((s_ref[...] - mcol).astype(BF16))
        acc = _dot(v_all[...], p)
        o = acc[:hd2] / acc[hd2:hd2 + 1]
        d = (o[:, :tq] - lam * o[:, tq:]).T
        y = _rms(d, g_ref[...]) * (1.0 - lam_init)
        o_ref[0, pl.ds(pl.multiple_of(t * tq, tq), tq), :] = y.astype(BF16)

    def triple(i, carry):
        m_a, m_b = carry
        t = 3 * i
        m_c = scores(t + 2, sc_ref)
        attend(t, sa_ref, m_a)
        m_a = scores(t + 3, sa_ref)
        attend(t + 1, sb_ref, m_b)
        m_b = scores(jnp.minimum(t + 4, nq - 1), sb_ref)
        attend(t + 2, sc_ref, m_c)
        return m_a, m_b

    m_a, _ = lax.fori_loop(0, (nq - 1) // 3, triple, (scores(0, sa_ref), scores(1, sb_ref)))
    attend(nq - 1, sa_ref, m_a)


def _diff_attn(dq, dk, vt, dkc, vtc, lam_vec, subln_g, lam_init):
    nbatch, seq, _ = dq.shape
    ctx_len = dkc.shape[1]
    nkeys = ctx_len + seq
    hd2 = 2 * HEAD_DIM
    assert (seq // Q_BLOCK - 1) % 3 == 0
    return pl.pallas_call(
        functools.partial(_diff_attn_kernel, lam_init),
        out_shape=jax.ShapeDtypeStruct((nbatch, seq, DIFF_V), BF16),
        grid=(nbatch, H_DIFF),
        in_specs=[pl.BlockSpec((1, seq, hd2), lambda b, h: (b, 0, h)),
                  pl.BlockSpec((1, seq, hd2), lambda b, h: (b, 0, h)),
                  pl.BlockSpec((1, ctx_len, hd2), lambda b, h: (b, 0, h)),
                  pl.BlockSpec((1, hd2, seq), lambda b, h: (b, h, 0)),
                  pl.BlockSpec((1, hd2, ctx_len), lambda b, h: (b, h, 0)),
                  _const_spec(lam_vec.shape), _const_spec((1, hd2))],
        out_specs=pl.BlockSpec((1, seq, hd2), lambda b, h: (b, 0, h)),
        scratch_shapes=[pltpu.VMEM((nkeys, hd2), BF16),
                        pltpu.VMEM((hd2 + V7X_BF16_SUBLANES, nkeys), BF16),
                        *[pltpu.VMEM((nkeys, 2 * Q_BLOCK), F32) for _ in range(3)]],
        compiler_params=_cparams("arbitrary", "arbitrary"),
        name="diff_attention",
    )(dq, dk, dkc, vt, vtc, lam_vec, subln_g)


WIN_GW = WIN_G * HEAD_DIM
WIN_REP_ROWS = 512


def _win_attn_kernel(q_ref, k_ref, v_ref, kc_ref, vc_ref, rep_ref, bias_ref, sink_ref, o_ref,
                     kr_ref, vr_ref, kcr_ref, vcr_ref, *s_refs):
    seq = q_ref.shape[1]
    ctx_len = kc_ref.shape[1]
    tq = Q_BLOCK
    nq = seq // tq
    span = tq + 2 * WINDOW
    kvh = pl.program_id(1)
    rep = rep_ref[0]

    def widen(c, carry):
        r0 = pl.multiple_of(c * WIN_REP_ROWS, WIN_REP_ROWS)
        kr_ref[pl.ds(r0, WIN_REP_ROWS), :] = _dot(k_ref[0, pl.ds(r0, WIN_REP_ROWS), :], rep).astype(BF16)
        vr_ref[pl.ds(r0, WIN_REP_ROWS), :] = _dot(v_ref[0, pl.ds(r0, WIN_REP_ROWS), :], rep).astype(BF16)
        return carry

    lax.fori_loop(0, seq // WIN_REP_ROWS, widen, 0)
    kcr_ref[...] = _dot(kc_ref[0], rep).astype(BF16)
    vcr_ref[...] = _dot(vc_ref[0], rep).astype(BF16)
    lane_group = lax.broadcasted_iota(jnp.int32, (1, WIN_GW), 1) >> 6
    zero = jnp.zeros((), BF16)
    grow = lax.broadcasted_iota(jnp.int32, (WIN_G * tq, 1), 0) >> 7
    sink = jnp.zeros((WIN_G * tq, 1), F32)
    for g in range(WIN_G):
        sink = jnp.where(grow == g, sink_ref[kvh * WIN_G + g], sink)

    def window_start(t):
        return pl.multiple_of(jnp.clip(t * tq - WINDOW, 0, seq - span), tq)

    def scores(t, s_ref):
        t = jnp.minimum(t, nq - 1)
        r0 = pl.multiple_of(t * tq, tq)
        edge = jnp.where(t == 0, 0, jnp.where(t == nq - 1, 2, 1))
        q = q_ref[0, pl.ds(r0, tq), :]
        q4 = jnp.concatenate([jnp.where(lane_group == g, q, zero) for g in range(WIN_G)], axis=0)
        s_c = _dot_nt(q4, kcr_ref[...])
        s_w = _dot_nt(q4, kr_ref[pl.ds(window_start(t), span), :]) + bias_ref[edge]
        s_ref[:, 0:ctx_len] = s_c
        s_ref[:, ctx_len:ctx_len + span] = s_w
        m = jnp.maximum(jnp.maximum(jnp.max(s_c, axis=-1, keepdims=True),
                                    jnp.max(s_w, axis=-1, keepdims=True)), sink)
        s_ref[:, ctx_len + span:] = jnp.broadcast_to(m, (WIN_G * tq, V7X_LANES))

    def attend(t, s_ref):
        m = s_ref[:, ctx_len + span:ctx_len + span + 1]
        e_c = jnp.exp(s_ref[:, 0:ctx_len] - m)
        e_w = jnp.exp(s_ref[:, ctx_len:ctx_len + span] - m)
        denom = (jnp.sum(e_c, axis=-1, keepdims=True) + jnp.sum(e_w, axis=-1, keepdims=True)
                 + jnp.exp(sink - m))
        o4 = (_dot(e_c.astype(BF16), vcr_ref[...])
              + _dot(e_w.astype(BF16), vr_ref[pl.ds(window_start(t), span), :])) / denom
        out = jnp.zeros((tq, WIN_GW), F32)
        for g in range(WIN_G):
            out = out + jnp.where(lane_group == g, o4[g * tq:(g + 1) * tq], 0.0)
        o_ref[0, pl.ds(pl.multiple_of(t * tq, tq), tq), :] = out.astype(BF16)

    def quad(i, carry):
        t = 4 * i
        scores(t + 2, s_refs[2])
        scores(t + 3, s_refs[3])
        attend(t, s_refs[0])
        attend(t + 1, s_refs[1])
        scores(t + 4, s_refs[0])
        scores(t + 5, s_refs[1])
        attend(t + 2, s_refs[2])
        attend(t + 3, s_refs[3])
        return carry

    scores(0, s_refs[0])
    scores(1, s_refs[1])
    lax.fori_loop(0, nq // 4, quad, 0)


def _win_attn(wq, wk, wv, wkc, wvc, rep, bias, sink):
    nbatch, seq, _ = wq.shape
    ctx_len = wkc.shape[1]
    assert (seq // Q_BLOCK) % 4 == 0
    span = Q_BLOCK + 2 * WINDOW
    srows, scols = WIN_G * Q_BLOCK, ctx_len + span + V7X_LANES
    per_b = lambda b, j: (b, 0, 0)
    per_bh = lambda b, j: (b, 0, j)
    return pl.pallas_call(
        _win_attn_kernel,
        out_shape=jax.ShapeDtypeStruct((nbatch, seq, WIN_Q), BF16),
        grid=(nbatch, H_WIN_KV),
        in_specs=[pl.BlockSpec((1, seq, WIN_GW), per_bh),
                  pl.BlockSpec((1, seq, WIN_KV), per_b),
                  pl.BlockSpec((1, seq, WIN_KV), per_b),
                  pl.BlockSpec((1, ctx_len, WIN_KV), per_b),
                  pl.BlockSpec((1, ctx_len, WIN_KV), per_b),
                  pl.BlockSpec((1, WIN_KV, WIN_GW), lambda b, j: (j, 0, 0)),
                  _const_spec(bias.shape),
                  pl.BlockSpec(memory_space=pltpu.SMEM)],
        out_specs=pl.BlockSpec((1, seq, WIN_GW), per_bh),
        scratch_shapes=[pltpu.VMEM((seq, WIN_GW), BF16), pltpu.VMEM((seq, WIN_GW), BF16),
                        pltpu.VMEM((ctx_len, WIN_GW), BF16), pltpu.VMEM((ctx_len, WIN_GW), BF16),
                        *[pltpu.VMEM((srows, scols), F32) for _ in range(4)]],
        compiler_params=_cparams("arbitrary", "arbitrary"),
        name="window_attention",
    )(wq, wk, wv, wkc, wvc, rep, bias, sink)


def _cos_sin(num, den):
    ang = 2.0 * np.pi * (np.asarray(num, np.int64) % den).astype(np.float64) / den
    return np.cos(ang), np.sin(ang)


def _channel_dft_table():
    k = np.arange(FOURIER_GW)
    c, s = _cos_sin(np.outer(k, k), FOURIER_GW)
    return np.concatenate([c, -s], axis=1)


def _stage_a_table(n1c):
    k = np.arange(n1c)
    c, s = _cos_sin(np.outer(k, k), n1c)
    return np.kron(np.block([[c, s], [-s, c]]), np.eye(FFT_RB))


def _stage_b_table(norm):
    k = np.arange(V7X_LANES)
    c, s = _cos_sin(np.outer(k, k), V7X_LANES)
    kb = np.zeros((V7X_LANES, FFT_RB, 2, FFT_RB, V7X_LANES))
    for i in range(FFT_RB):
        kb[:, i, 0, i, :] = c / norm
        kb[:, i, 1, i, :] = s / norm
    return kb.reshape(V7X_LANES * FFT_RB, 2 * FFT_RB * V7X_LANES)


def _twiddle_tables(seq):
    n1c = seq // V7X_LANES
    c, s = _cos_sin(np.outer(np.arange(n1c), np.arange(V7X_LANES)), seq)
    expand = lambda t: np.repeat(t[:, :, None], V7X_LANES, axis=2)
    return expand(c), expand(s)


def _real_dft_table(n, norm):
    k = np.arange(n)
    c, s = _cos_sin(np.outer(k, k), n)
    return np.concatenate([c, s], axis=1) / norm


def _rope_tables(seq):
    t = np.arange(seq)
    axis_dim = HEAD_DIM // 2
    inv_freq = ROPE_BASE ** (-np.arange(0, axis_dim, 2, dtype=np.float64) / axis_dim)
    ang_r = (t // GRID_W).astype(np.float64)[:, None] * inv_freq[None, :]
    ang_c = (t % GRID_W).astype(np.float64)[:, None] * inv_freq[None, :]
    cos = np.concatenate([np.cos(ang_r)] * 2 + [np.cos(ang_c)] * 2, axis=1)
    sin = np.concatenate([-np.sin(ang_r), np.sin(ang_r), -np.sin(ang_c), np.sin(ang_c)], axis=1)
    reps = V7X_LANES // HEAD_DIM
    return np.tile(cos, (1, reps)), np.tile(sin, (1, reps))


def _window_bias_table():
    span = Q_BLOCK + 2 * WINDOW
    q = np.arange(Q_BLOCK)[:, None]
    k = np.arange(span)[None, :]
    out = []
    for start in (0, WINDOW, 2 * WINDOW):
        vis = np.abs(k - start - q) <= WINDOW
        out.append(np.tile(np.where(vis, 0.0, NEG_INF), (WIN_G, 1)))
    return np.stack(out).astype(np.float32)


def _replication_table():
    rep = np.zeros((H_WIN_KV, WIN_KV, WIN_GW), np.float32)
    for j in range(H_WIN_KV):
        for g in range(WIN_G):
            for d in range(HEAD_DIM):
                rep[j, j * HEAD_DIM + d, g * HEAD_DIM + d] = 1.0
    return rep


def _fourier_conv_mix(h2d, mod, gpre, w_in, conv_w, conv_b, ln_g, ln_b, nbatch, tm):
    t = h2d.shape[0]
    seq = t // nbatch
    table = lambda a: jnp.asarray(a, F32).astype(BF16)
    cw = table(_channel_dft_table())
    norm = math.sqrt(seq * FOURIER_GW)
    if seq % (FFT_RB * V7X_LANES) == 0:
        n1c = seq // V7X_LANES
        twc, tws = _twiddle_tables(seq)
        z, u = _ev_fft_a(h2d, mod, gpre, w_in, cw, table(_stage_a_table(n1c)),
                         jnp.asarray(twc, F32), jnp.asarray(tws, F32), nbatch)
        fa = _ev_fft_b(z, table(_stage_b_table(norm)))
        u = u.reshape(t, D_CONV)
    else:
        p, u = _ev_proj(h2d, mod, gpre, w_in, cw, nbatch, tm)
        fa = _dft_real(p.reshape(nbatch, 2 * seq, D_FOURIER), table(_real_dft_table(seq, norm)),
                       D_FOURIER)
    fa = fa.reshape(t, D_FOURIER)
    uc = _conv_ln_swish(u, conv_w, conv_b, ln_g, ln_b, nbatch, tm)
    return fa, uc


def _diff_window_mix(x2d, c2d, modx, modc, gpre, w_in, lam_vec, subln_g, sink, lam_init, nbatch):
    t = x2d.shape[0]
    seq = t // nbatch
    ctx_len = c2d.shape[0] // nbatch
    cos, sin = _rope_tables(seq)
    dq, dk, vt, wq, wk, wv = _od_proj(x2d, modx, gpre, w_in, jnp.asarray(cos, F32),
                                      jnp.asarray(sin, F32), nbatch, KV_CHUNK)
    dkc, vtc, wkc, wvc = _od_proj_ctx(c2d, modc, gpre, w_in, nbatch)
    b3 = lambda a, n: a.reshape(nbatch, n, a.shape[-1])
    diff = _diff_attn(b3(dq, seq), b3(dk, seq), vt, b3(dkc, ctx_len), vtc, lam_vec, subln_g, lam_init)
    win = _win_attn(b3(wq, seq), b3(wk, seq), b3(wv, seq), b3(wkc, ctx_len), b3(wvc, ctx_len),
                    jnp.asarray(_replication_table(), F32).astype(BF16),
                    jnp.asarray(_window_bias_table(), F32), sink)
    return diff.reshape(t, DIFF_V), win.reshape(t, WIN_Q)


def kernel(x, c, ctx, c_ctx, w_mod, b_mod, norm_pre, norm_post, ffn_w_in, ffn_w_out, ev_w_in, ev_conv_w, ev_conv_b, ev_ln_g, ev_ln_b, ev_w_out, od_w_in, od_lambda, od_subln_g, od_sink, od_w_out):
    nbatch, seq, d = x.shape
    ctx_len = ctx.shape[1]
    depth = w_mod.shape[0]
    assert d == D_MODEL and depth == DEPTH == 2 and ctx_len == KV_CHUNK
    assert seq % FFN_TM == 0 and seq % (8 * V7X_LANES) == 0

    mod_rows = 8
    cs = jnp.concatenate([c, c_ctx[None, :], jnp.zeros((mod_rows - nbatch - 1, d), F32)], axis=0)
    mod = _modulation(cs, w_mod, b_mod).reshape(depth, mod_rows, N_MOD, d)
    row = lambda v: v.reshape(1, -1)

    x2d = x.reshape(nbatch * seq, d)
    c2d = ctx.reshape(nbatch * ctx_len, d)
    ffn_in = ffn_w_in.astype(BF16)
    ffn_out = ffn_w_out.astype(BF16)

    def ffn_x(x2d, l, s, k):
        return _ffn(x2d, mod[l, :nbatch], row(norm_pre[l, s]), row(norm_post[l, s]),
                    ffn_in, ffn_out, (l, k), s, FFN_TM, seq // FFN_TM)

    def ffn_c(c2d, l, s, k):
        return _ffn(c2d, mod[l, nbatch:nbatch + 1], row(norm_pre[l, s]), row(norm_post[l, s]),
                    ffn_in, ffn_out, (l, k), s, ctx_len, 1)

    def mix_ffn(branches, w_mix, h2d, l, m, tm, tpb):
        return _mix_ffn(*branches, w_mix, row(norm_post[l, 1]), h2d, m, row(norm_pre[l, 2]),
                        row(norm_post[l, 2]), ffn_in, ffn_out, (l, 1), 2, tm, tpb)

    l = 0
    modx, modc = mod[l, :nbatch], mod[l, nbatch:nbatch + 1]
    x2d = ffn_x(x2d, l, 0, 0)
    c2d = ffn_c(c2d, l, 0, 0)
    ev_args = (row(norm_pre[l, 1]), ev_w_in[0].astype(BF16), ev_conv_w[0],
               row(ev_conv_b[0]), row(ev_ln_g[0]), row(ev_ln_b[0]))
    ev_out = ev_w_out[0].astype(BF16)
    x2d = mix_ffn(_fourier_conv_mix(x2d, modx, *ev_args, nbatch, ROW_TM), ev_out, x2d, l, modx,
                  FFN_TM, seq // FFN_TM)
    c2d = mix_ffn(_fourier_conv_mix(c2d, modc, *ev_args, nbatch, ctx_len), ev_out, c2d, l, modc,
                  ctx_len, 1)

    l = 1
    modx, modc = mod[l, :nbatch], mod[l, nbatch:nbatch + 1]
    x2d = ffn_x(x2d, l, 0, 0)
    c2d = ffn_c(c2d, l, 0, 0)
    lam_init = 0.8 - 0.6 * math.exp(-0.3 * l)
    branches = _diff_window_mix(x2d, c2d, modx, modc, row(norm_pre[l, 1]), od_w_in[0].astype(BF16),
                                od_lambda[0], row(od_subln_g[0]), od_sink[0], lam_init, nbatch)
    x2d = mix_ffn(branches, od_w_out[0].astype(BF16), x2d, l, modx, FFN_TM, seq // FFN_TM)
    return x2d.reshape(nbatch, seq, d)
```

```python
import functools
import math

import numpy as np
import jax
import jax.numpy as jnp
from jax import lax
from jax.experimental import pallas as pl
from jax.experimental.pallas import tpu as pltpu

F32 = jnp.float32
BF16 = jnp.bfloat16

D_MODEL = 1024
DEPTH = 2
GRID_W = 64
N_MOD = 9
EPS = 1e-6
NEG_INF = -1e30
HALF_STEP = 0.5
D_FF = 2816
FOURIER_GROUPS = 4
FOURIER_GW = D_MODEL // 8
D_FOURIER = FOURIER_GROUPS * FOURIER_GW
D_CONV = D_MODEL // 2
CONV_W = 31
HEAD_DIM = 64
ROPE_BASE = 10000.0
H_DIFF = 4
DIFF_QK = H_DIFF * 2 * HEAD_DIM
DIFF_V = H_DIFF * 2 * HEAD_DIM
H_WIN = 8
H_WIN_KV = 2
WIN_G = H_WIN // H_WIN_KV
WIN_Q = H_WIN * HEAD_DIM
WIN_KV = H_WIN_KV * HEAD_DIM
WINDOW = 128
Q_BLOCK = 128

V7X_LANES = 128
V7X_MXU_DIM = 256
V7X_BF16_SUBLANES = 16
V7X_VMEM_LIMIT_BYTES = 56 * 1024 * 1024

FFN_TM = 1024
FFN_SUB = 512
FFN_FC = V7X_MXU_DIM
ROW_TM = 256
LOG2_E = math.log2(math.e)
KV_CHUNK = 256
FFT_RB = 8
CONV_RB = 32
CONV_HALO = V7X_BF16_SUBLANES
CONV_BLK = 128
CONV_SPAN = CONV_BLK + 2 * CONV_HALO
CONV_K = V7X_MXU_DIM


def _cparams(*sem):
    return pltpu.CompilerParams(dimension_semantics=sem, vmem_limit_bytes=V7X_VMEM_LIMIT_BYTES)


def _const_spec(shape):
    n = len(shape)
    return pl.BlockSpec(shape, lambda *_: (0,) * n, pipeline_mode=pl.Buffered(1))


def _dot(a, b):
    return jnp.dot(a, b, preferred_element_type=F32)


def _dot_nt(a, b):
    return lax.dot_general(a, b, (((1,), (1,)), ((), ())), preferred_element_type=F32)


def _rms(xf, g):
    return xf * lax.rsqrt(jnp.mean(xf * xf, axis=-1, keepdims=True) + EPS) * g


def _mrow(m_ref, j):
    return m_ref[0, j:j + 1, :]


def _pre(x, g, m_ref, s):
    return _rms(x, g) * (1.0 + _mrow(m_ref, 3 * s + 1)) + _mrow(m_ref, 3 * s)


def _mod_spec(mod, tiles_per_batch):
    d = mod.shape[-1]
    if mod.shape[0] == 1:
        return pl.BlockSpec((1, N_MOD, d), lambda i: (0, 0, 0))
    return pl.BlockSpec((1, N_MOD, d), lambda i: (i // tiles_per_batch, 0, 0))


def _sigmoid(x):
    return 1.0 / (1.0 + jnp.exp(-x))


def _mod_kernel(c_ref, w_ref, b_ref, o_ref):
    c = c_ref[...]
    h = c * _sigmoid(c)
    o_ref[0] = jnp.dot(h, w_ref[0], preferred_element_type=F32,
                       precision=lax.Precision.HIGHEST) + b_ref[0]


def _modulation(cs, w_mod, b_mod):
    depth, d, nd = w_mod.shape
    rows = cs.shape[0]
    return pl.pallas_call(
        _mod_kernel,
        out_shape=jax.ShapeDtypeStruct((depth, rows, nd), F32),
        grid=(depth, nd // d),
        in_specs=[pl.BlockSpec((rows, d), lambda l, j: (0, 0)),
                  pl.BlockSpec((1, d, d), lambda l, j: (l, 0, j)),
                  pl.BlockSpec((1, 1, d), lambda l, j: (l, 0, j))],
        out_specs=pl.BlockSpec((1, rows, d), lambda l, j: (l, 0, j)),
        compiler_params=_cparams("arbitrary", "arbitrary"),
        name="modulation",
    )(cs, w_mod, b_mod.reshape(depth, 1, nd))


def _swiglu_rows(s, x, rows, m_ref, gpre_ref, gpost_ref, win_ref, wout_ref, o_ref, a_ref):
    h = _pre(x, gpre_ref[...], m_ref, s).astype(BF16)
    for c in range(D_FF // FFN_FC):
        lo = c * FFN_FC
        g = _dot(h, win_ref[:, lo:lo + FFN_FC])
        u = _dot(h, win_ref[:, D_FF + lo:D_FF + lo + FFN_FC])
        a_ref[rows, lo:lo + FFN_FC] = (g * _sigmoid(g) * u).astype(BF16)
    y = _dot(a_ref[rows, :], wout_ref[...])
    o_ref[rows, :] = x + HALF_STEP * _mrow(m_ref, 3 * s + 2) * _rms(y, gpost_ref[...])


def _row_subtiles(n):
    sub = min(FFN_SUB, n)
    return [slice(j * sub, (j + 1) * sub) for j in range(n // sub)]


def _ffn_kernel(s, x_ref, m_ref, gpre_ref, gpost_ref, win_ref, wout_ref, o_ref, a_ref):
    for rows in _row_subtiles(x_ref.shape[0]):
        _swiglu_rows(s, x_ref[rows, :], rows, m_ref, gpre_ref, gpost_ref, win_ref, wout_ref, o_ref, a_ref)


def _mix_ffn_kernel(s, ma_ref, mb_ref, wmix_ref, gmix_ref, x_ref, m_ref, gpre_ref, gpost_ref,
                    win_ref, wout_ref, o_ref, a_ref):
    ka = ma_ref.shape[1]
    for rows in _row_subtiles(x_ref.shape[0]):
        y = (_dot(ma_ref[rows, :].astype(BF16), wmix_ref[:ka])
             + _dot(mb_ref[rows, :].astype(BF16), wmix_ref[ka:]))
        x = x_ref[rows, :] + _mrow(m_ref, 5) * _rms(y, gmix_ref[...])
        _swiglu_rows(s, x, rows, m_ref, gpre_ref, gpost_ref, win_ref, wout_ref, o_ref, a_ref)


def _stacked_weight_spec(w, lk):
    l, k = lk
    return pl.BlockSpec((None, None) + w.shape[2:], lambda i: (l, k, 0, 0),
                        pipeline_mode=pl.Buffered(1))


def _ffn(x2d, mod, gpre, gpost, w_in, w_out, lk, s, tm, tiles_per_batch):
    t, d = x2d.shape
    return pl.pallas_call(
        functools.partial(_ffn_kernel, s),
        out_shape=jax.ShapeDtypeStruct((t, d), F32),
        grid=(t // tm,),
        in_specs=[pl.BlockSpec((tm, d), lambda i: (i, 0)),
                  _mod_spec(mod, tiles_per_batch),
                  _const_spec((1, d)), _const_spec((1, d)),
                  _stacked_weight_spec(w_in, lk), _stacked_weight_spec(w_out, lk)],
        out_specs=pl.BlockSpec((tm, d), lambda i: (i, 0)),
        scratch_shapes=[pltpu.VMEM((tm, D_FF), BF16)],
        compiler_params=_cparams("arbitrary"),
        name="swiglu_halfstep",
    )(x2d, mod, gpre, gpost, w_in, w_out)


def _mix_ffn(ma, mb, w_mix, g_mix, x2d, mod, gpre, gpost, w_in, w_out, lk, s, tm, tiles_per_batch):
    t, d = x2d.shape
    row = lambda i: (i, 0)
    return pl.pallas_call(
        functools.partial(_mix_ffn_kernel, s),
        out_shape=jax.ShapeDtypeStruct((t, d), F32),
        grid=(t // tm,),
        in_specs=[pl.BlockSpec((tm, ma.shape[1]), row), pl.BlockSpec((tm, mb.shape[1]), row),
                  _const_spec(w_mix.shape), _const_spec((1, d)),
                  pl.BlockSpec((tm, d), row),
                  _mod_spec(mod, tiles_per_batch),
                  _const_spec((1, d)), _const_spec((1, d)),
                  _stacked_weight_spec(w_in, lk), _stacked_weight_spec(w_out, lk)],
        out_specs=pl.BlockSpec((tm, d), row),
        scratch_shapes=[pltpu.VMEM((tm, D_FF), BF16)],
        compiler_params=_cparams("arbitrary"),
        name="outproj_swiglu_halfstep",
    )(ma, mb, w_mix, g_mix, x2d, mod, gpre, gpost, w_in, w_out)


def _ev_proj_kernel(x_ref, m_ref, gpre_ref, w_ref, cw_ref, p_ref, u_ref):
    h = _pre(x_ref[...], gpre_ref[...], m_ref, 1).astype(BF16)
    a = _dot(h, w_ref[:, :D_FOURIER])
    g1 = _dot(h, w_ref[:, D_FOURIER:D_FOURIER + D_CONV])
    g2 = _dot(h, w_ref[:, D_FOURIER + D_CONV:])
    u_ref[...] = (g1 * _sigmoid(g2)).astype(BF16)
    gw = FOURIER_GW
    for g in range(FOURIER_GROUPS):
        pg = _dot(a[:, g * gw:(g + 1) * gw].astype(BF16), cw_ref[...])
        p_ref[0, 0, :, g * gw:(g + 1) * gw] = pg[:, :gw].astype(BF16)
        p_ref[0, 1, :, g * gw:(g + 1) * gw] = pg[:, gw:].astype(BF16)


def _ev_proj(x2d, mod, gpre, w_in, cw, nbatch, tm):
    t, d = x2d.shape
    seq = t // nbatch
    tpb = seq // tm
    return pl.pallas_call(
        _ev_proj_kernel,
        out_shape=(jax.ShapeDtypeStruct((nbatch, 2, seq, D_FOURIER), BF16),
                   jax.ShapeDtypeStruct((t, D_CONV), BF16)),
        grid=(t // tm,),
        in_specs=[pl.BlockSpec((tm, d), lambda i: (i, 0)),
                  _mod_spec(mod, tpb),
                  _const_spec((1, d)), _const_spec(w_in.shape), _const_spec(cw.shape)],
        out_specs=(pl.BlockSpec((1, 2, tm, D_FOURIER), lambda i: (i // tpb, 0, i % tpb, 0)),
                   pl.BlockSpec((tm, D_CONV), lambda i: (i, 0))),
        compiler_params=_cparams("arbitrary"),
        name="even_proj",
    )(x2d, mod, gpre, w_in, cw)


def _ev_fft_a_kernel(x_ref, m_ref, gpre_ref, w_ref, cw_ref, ka_ref, twc_ref, tws_ref, z_ref, u_ref):
    n1c = x_ref.shape[1]
    rows = n1c * FFT_RB
    x = x_ref[0].reshape(rows, D_MODEL)
    h = _pre(x, gpre_ref[...], m_ref, 1).astype(BF16)
    a = _dot(h, w_ref[:, :D_FOURIER])
    g1 = _dot(h, w_ref[:, D_FOURIER:D_FOURIER + D_CONV])
    g2 = _dot(h, w_ref[:, D_FOURIER + D_CONV:])
    u_ref[0] = (g1 * _sigmoid(g2)).reshape(n1c, FFT_RB, D_CONV)
    gw = FOURIER_GW
    pr, pi = [], []
    for g in range(FOURIER_GROUPS):
        pg = _dot(a[:, g * gw:(g + 1) * gw].astype(BF16), cw_ref[...])
        pr.append(pg[:, :gw])
        pi.append(pg[:, gw:])
    p = jnp.concatenate([jnp.concatenate(pr, axis=1), jnp.concatenate(pi, axis=1)], axis=0)
    z = _dot(ka_ref[...], p.astype(BF16))
    zr, zi = z[:rows], z[rows:]
    tc = jnp.concatenate([twc_ref[...].reshape(rows, V7X_LANES)] * FOURIER_GROUPS, axis=1)
    ts = jnp.concatenate([tws_ref[...].reshape(rows, V7X_LANES)] * FOURIER_GROUPS, axis=1)
    z_ref[0, 0] = (zr * tc + zi * ts).reshape(n1c, FFT_RB, D_FOURIER)
    z_ref[0, 1] = (zi * tc - zr * ts).reshape(n1c, FFT_RB, D_FOURIER)


def _ev_fft_a(x2d, mod, gpre, w_in, cw, ka, twc, tws, nbatch):
    t, d = x2d.shape
    seq = t // nbatch
    n1c = seq // V7X_LANES
    nrb = V7X_LANES // FFT_RB
    x4 = x2d.reshape(nbatch, n1c, V7X_LANES, d)
    blk = lambda w: pl.BlockSpec((1, n1c, FFT_RB, w), lambda i: (i // nrb, 0, i % nrb, 0))
    return pl.pallas_call(
        _ev_fft_a_kernel,
        out_shape=(jax.ShapeDtypeStruct((nbatch, 2, n1c, V7X_LANES, D_FOURIER), F32),
                   jax.ShapeDtypeStruct((nbatch, n1c, V7X_LANES, D_CONV), F32)),
        grid=(nbatch * nrb,),
        in_specs=[blk(d), _mod_spec(mod, nrb), _const_spec((1, d)), _const_spec(w_in.shape),
                  _const_spec(cw.shape), _const_spec(ka.shape),
                  pl.BlockSpec((n1c, FFT_RB, V7X_LANES), lambda i: (0, i % nrb, 0)),
                  pl.BlockSpec((n1c, FFT_RB, V7X_LANES), lambda i: (0, i % nrb, 0))],
        out_specs=(pl.BlockSpec((1, 2, n1c, FFT_RB, D_FOURIER), lambda i: (i // nrb, 0, 0, i % nrb, 0)),
                   blk(D_CONV)),
        compiler_params=_cparams("arbitrary"),
        name="even_proj_fft_a",
    )(x4, mod, gpre, w_in, cw, ka, twc, tws)


def _ev_fft_b_kernel(kb_ref, z_ref, o_ref):
    z = z_ref[0].reshape(2 * FFT_RB * V7X_LANES, D_FOURIER).astype(BF16)
    o_ref[0] = _dot(kb_ref[...], z).reshape(V7X_LANES, FFT_RB, D_FOURIER)


def _ev_fft_b(z, kb):
    nbatch, _, n1c, _, nc = z.shape
    return pl.pallas_call(
        _ev_fft_b_kernel,
        out_shape=jax.ShapeDtypeStruct((nbatch, V7X_LANES, n1c, nc), F32),
        grid=(nbatch, n1c // FFT_RB),
        in_specs=[_const_spec(kb.shape),
                  pl.BlockSpec((1, 2, FFT_RB, V7X_LANES, nc), lambda b, j: (b, 0, j, 0, 0))],
        out_specs=pl.BlockSpec((1, V7X_LANES, FFT_RB, nc), lambda b, j: (b, 0, j, 0)),
        compiler_params=_cparams("arbitrary", "arbitrary"),
        name="fft_b",
    )(kb, z)


def _dft_real_kernel(mat_ref, z_ref, o_ref):
    o_ref[0] = _dot(mat_ref[...], z_ref[0]).astype(o_ref.dtype)


def _dft_real(z, mat, nb):
    nbatch, k, n = z.shape
    m = mat.shape[0]
    return pl.pallas_call(
        _dft_real_kernel,
        out_shape=jax.ShapeDtypeStruct((nbatch, m, n), BF16),
        grid=(nbatch, n // nb),
        in_specs=[_const_spec(mat.shape), pl.BlockSpec((1, k, nb), lambda b, j: (b, 0, j))],
        out_specs=pl.BlockSpec((1, m, nb), lambda b, j: (b, 0, j)),
        compiler_params=_cparams("arbitrary", "arbitrary"),
        name="dft_real",
    )(mat, z)


def _conv_kernel(tpb, u_ref, up_ref, un_ref, sh_ref, cw_ref, cb_ref, lg_ref, lb_ref, o_ref,
                 ext_ref, cp_ref):
    tm = u_ref.shape[0]
    j = pl.program_id(0) % tpb
    halo = CONV_HALO
    ext_ref[0:halo] = jnp.where(j > 0, up_ref[...], 0).astype(BF16)
    ext_ref[halo:halo + tm] = u_ref[...].astype(BF16)
    ext_ref[halo + tm:tm + 2 * halo] = jnp.where(j < tpb - 1, un_ref[...], 0).astype(BF16)
    ext_ref[tm + 2 * halo:] = jnp.zeros((CONV_K - CONV_SPAN, D_CONV), BF16)
    cb, lg, lb = cb_ref[...], lg_ref[...], lb_ref[...]
    for blk in range(tm // CONV_BLK):
        b0 = blk * CONV_BLK
        cp_ref[blk] = _dot(sh_ref[...], ext_ref[b0:b0 + CONV_K, :])
        for c in range(CONV_BLK // CONV_RB):
            base = c * CONV_RB
            acc = None
            for t in range(CONV_W):
                r, off = t % 8, t - t % 8
                rows = cp_ref[blk, r * CONV_SPAN + base + off:r * CONV_SPAN + base + off + CONV_RB, :]
                term = cw_ref[t:t + 1, :] * rows
                acc = term if acc is None else acc + term
            v = acc + cb
            vc = v - jnp.mean(v, axis=-1, keepdims=True)
            var = jnp.mean(vc * vc, axis=-1, keepdims=True)
            y = vc * lax.rsqrt(var + EPS) * lg + lb
            o_ref[b0 + base:b0 + base + CONV_RB, :] = (y * _sigmoid(y)).astype(BF16)


def _conv_shift_table():
    first = CONV_HALO - CONV_W // 2
    sh = np.zeros((8, CONV_SPAN, CONV_K), np.float32)
    for r in range(8):
        for m in range(CONV_SPAN - first - r):
            sh[r, m, m + first + r] = 1.0
    return sh.reshape(8 * CONV_SPAN, CONV_K)


def _conv_ln_swish(u, conv_w, conv_b, ln_g, ln_b, nbatch, tm):
    t, dc = u.shape
    tpb = (t // nbatch) // tm
    hb = tm // CONV_HALO
    last = t // CONV_HALO - 1
    shift = jnp.asarray(_conv_shift_table(), F32).astype(BF16)
    return pl.pallas_call(
        functools.partial(_conv_kernel, tpb),
        out_shape=jax.ShapeDtypeStruct((t, dc), BF16),
        grid=(t // tm,),
        in_specs=[pl.BlockSpec((tm, dc), lambda i: (i, 0)),
                  pl.BlockSpec((CONV_HALO, dc), lambda i: (jnp.maximum(i * hb - 1, 0), 0)),
                  pl.BlockSpec((CONV_HALO, dc), lambda i: (jnp.minimum((i + 1) * hb, last), 0)),
                  _const_spec(shift.shape),
                  _const_spec(conv_w.shape), _const_spec((1, dc)), _const_spec((1, dc)),
                  _const_spec((1, dc))],
        out_specs=pl.BlockSpec((tm, dc), lambda i: (i, 0)),
        scratch_shapes=[pltpu.VMEM((tm + CONV_K - CONV_BLK, dc), BF16),
                        pltpu.VMEM((tm // CONV_BLK, 8 * CONV_SPAN, dc), F32)],
        compiler_params=_cparams("arbitrary"),
        name="conv_ln_swish",
    )(u, u, u, shift, conv_w, conv_b, ln_g, ln_b)


def _rope(z, cos, sin, even_block):
    out = []
    for j in range(z.shape[1] // V7X_LANES):
        zs = z[:, j * V7X_LANES:(j + 1) * V7X_LANES]
        sw = jnp.where(even_block, pltpu.roll(zs, V7X_LANES - 16, 1), pltpu.roll(zs, 16, 1))
        out.append(zs * cos + sw * sin)
    return out[0] if len(out) == 1 else jnp.concatenate(out, axis=1)


def _store_transposed_chunks(o_ref, z):
    zt = z.T.astype(BF16)
    for c in range(z.shape[0] // V7X_LANES):
        o_ref[0, c] = zt[:, c * V7X_LANES:(c + 1) * V7X_LANES]


def _od_proj_kernel(x_ref, m_ref, gpre_ref, w_ref, cos_ref, sin_ref,
                    dq_ref, dk_ref, vt_ref, wq_ref, wk_ref, wv_ref):
    h = _pre(x_ref[...], gpre_ref[...], m_ref, 1).astype(BF16)
    cos, sin = cos_ref[...], sin_ref[...]
    lane = lax.broadcasted_iota(jnp.int32, (1, V7X_LANES), 1)
    even_block = ((lane >> 4) & 1) == 0
    scale = HEAD_DIM ** -0.5
    o = 0
    dq_ref[...] = (_rope(_dot(h, w_ref[:, o:o + DIFF_QK]), cos, sin, even_block)
                   * (scale * LOG2_E)).astype(BF16)
    o += DIFF_QK
    dk_ref[...] = _rope(_dot(h, w_ref[:, o:o + DIFF_QK]), cos, sin, even_block).astype(BF16)
    o += DIFF_QK
    vt_ref[0] = _dot(h, w_ref[:, o:o + DIFF_V]).T.astype(BF16)
    o += DIFF_V
    wq_ref[...] = (_rope(_dot(h, w_ref[:, o:o + WIN_Q]), cos, sin, even_block) * scale).astype(BF16)
    o += WIN_Q
    wk_ref[...] = _rope(_dot(h, w_ref[:, o:o + WIN_KV]), cos, sin, even_block).astype(BF16)
    o += WIN_KV
    _store_transposed_chunks(wv_ref, _dot(h, w_ref[:, o:o + WIN_KV]))


def _od_proj(x2d, mod, gpre, w_in, cos, sin, nbatch, tm):
    t, d = x2d.shape
    seq = t // nbatch
    tpb = seq // tm
    row = lambda i: (i, 0)
    return pl.pallas_call(
        _od_proj_kernel,
        out_shape=(jax.ShapeDtypeStruct((t, DIFF_QK), BF16),
                   jax.ShapeDtypeStruct((t, DIFF_QK), BF16),
                   jax.ShapeDtypeStruct((nbatch, DIFF_V, seq), BF16),
                   jax.ShapeDtypeStruct((t, WIN_Q), BF16),
                   jax.ShapeDtypeStruct((t, WIN_KV), BF16),
                   jax.ShapeDtypeStruct((nbatch, seq // V7X_LANES, WIN_KV, V7X_LANES), BF16)),
        grid=(t // tm,),
        in_specs=[pl.BlockSpec((tm, d), row),
                  _mod_spec(mod, tpb),
                  _const_spec((1, d)), _const_spec(w_in.shape),
                  pl.BlockSpec((tm, V7X_LANES), lambda i: (i % tpb, 0)),
                  pl.BlockSpec((tm, V7X_LANES), lambda i: (i % tpb, 0))],
        out_specs=(pl.BlockSpec((tm, DIFF_QK), row),
                   pl.BlockSpec((tm, DIFF_QK), row),
                   pl.BlockSpec((1, DIFF_V, tm), lambda i: (i // tpb, 0, i % tpb)),
                   pl.BlockSpec((tm, WIN_Q), row),
                   pl.BlockSpec((tm, WIN_KV), row),
                   pl.BlockSpec((1, tm // V7X_LANES, WIN_KV, V7X_LANES),
                                lambda i: (i // tpb, i % tpb, 0, 0))),
        compiler_params=_cparams("arbitrary"),
        name="odd_proj",
    )(x2d, mod, gpre, w_in, cos, sin)


def _od_proj_ctx_kernel(x_ref, m_ref, gpre_ref, w_ref, dk_ref, vt_ref, wk_ref, wv_ref):
    h = _pre(x_ref[...], gpre_ref[...], m_ref, 1).astype(BF16)
    o = DIFF_QK
    dk_ref[...] = _dot(h, w_ref[:, o:o + DIFF_QK]).astype(BF16)
    o += DIFF_QK
    vt_ref[0] = _dot(h, w_ref[:, o:o + DIFF_V]).T.astype(BF16)
    o += DIFF_V + WIN_Q
    wk_ref[...] = _dot(h, w_ref[:, o:o + WIN_KV]).astype(BF16)
    o += WIN_KV
    _store_transposed_chunks(wv_ref, _dot(h, w_ref[:, o:o + WIN_KV]))


def _od_proj_ctx(c2d, mod, gpre, w_in, nbatch):
    t, d = c2d.shape
    tm = t // nbatch
    row = lambda i: (i, 0)
    return pl.pallas_call(
        _od_proj_ctx_kernel,
        out_shape=(jax.ShapeDtypeStruct((t, DIFF_QK), BF16),
                   jax.ShapeDtypeStruct((nbatch, DIFF_V, tm), BF16),
                   jax.ShapeDtypeStruct((t, WIN_KV), BF16),
                   jax.ShapeDtypeStruct((nbatch, tm // V7X_LANES, WIN_KV, V7X_LANES), BF16)),
        grid=(nbatch,),
        in_specs=[pl.BlockSpec((tm, d), row), _const_spec(mod.shape),
                  _const_spec((1, d)), _const_spec(w_in.shape)],
        out_specs=(pl.BlockSpec((tm, DIFF_QK), row),
                   pl.BlockSpec((1, DIFF_V, tm), lambda i: (i, 0, 0)),
                   pl.BlockSpec((tm, WIN_KV), row),
                   pl.BlockSpec((1, tm // V7X_LANES, WIN_KV, V7X_LANES), lambda i: (i, 0, 0, 0))),
        compiler_params=_cparams("arbitrary"),
        name="odd_proj_ctx",
    )(c2d, mod, gpre, w_in)


def _diff_attn_kernel(lam_init, q_ref, kx_ref, kc_ref, vx_ref, vc_ref, lam_ref, g_ref, o_ref,
                      k_all, v_all, sa_ref, sb_ref, sc_ref):
    tq = Q_BLOCK
    seq = q_ref.shape[1]
    ctx_len = kc_ref.shape[1]
    nkeys = ctx_len + seq
    nq = seq // tq
    hd2 = 2 * HEAD_DIM
    k_all[0:ctx_len] = kc_ref[0]
    k_all[ctx_len:nkeys] = kx_ref[0]
    v_all[0:hd2, 0:ctx_len] = vc_ref[0]
    v_all[0:hd2, ctx_len:nkeys] = vx_ref[0]
    v_all[hd2:, :] = jnp.ones((V7X_BF16_SUBLANES, nkeys), BF16)
    lv = lam_ref[...]
    lam = (jnp.exp(jnp.sum(lv[0:1] * lv[1:2], axis=-1, keepdims=True))
           - jnp.exp(jnp.sum(lv[2:3] * lv[3:4], axis=-1, keepdims=True)) + lam_init)
    lane = lax.broadcasted_iota(jnp.int32, (1, hd2), 1)
    first_half = lane < HEAD_DIM
    zero = jnp.zeros((), BF16)

    def scores(t, s_ref):
        r0 = pl.multiple_of(t * tq, tq)
        q = q_ref[0, pl.ds(r0, tq), :]
        qq = jnp.concatenate([jnp.where(first_half, q, zero),
                              jnp.where(first_half, zero, q)], axis=0)
        s = _dot_nt(k_all[...], qq)
        s_ref[...] = s
        return jnp.max(s, axis=0, keepdims=True)

    def attend(t, s_ref, mcol):
        p = jnp.exp2((s_ref[...] - mcol).astype(BF16))
        acc = _dot(v_all[...], p)
        o = acc[:hd2] / acc[hd2:hd2 + 1]
        d = (o[:, :tq] - lam * o[:, tq:]).T
        y = _rms(d, g_ref[...]) * (1.0 - lam_init)
        o_ref[0, pl.ds(pl.multiple_of(t * tq, tq), tq), :] = y.astype(BF16)

    def triple(i, carry):
        m_a, m_b = carry
        t = 3 * i
        m_c = scores(t + 2, sc_ref)
        attend(t, sa_ref, m_a)
        m_a = scores(t + 3, sa_ref)
        attend(t + 1, sb_ref, m_b)
        m_b = scores(jnp.minimum(t + 4, nq - 1), sb_ref)
        attend(t + 2, sc_ref, m_c)
        return m_a, m_b

    m_a, _ = lax.fori_loop(0, (nq - 1) // 3, triple, (scores(0, sa_ref), scores(1, sb_ref)))
    attend(nq - 1, sa_ref, m_a)


def _diff_attn(dq, dk, vt, dkc, vtc, lam_vec, subln_g, lam_init):
    nbatch, seq, _ = dq.shape
    ctx_len = dkc.shape[1]
    nkeys = ctx_len + seq
    hd2 = 2 * HEAD_DIM
    assert (seq // Q_BLOCK - 1) % 3 == 0
    return pl.pallas_call(
        functools.partial(_diff_attn_kernel, lam_init),
        out_shape=jax.ShapeDtypeStruct((nbatch, seq, DIFF_V), BF16),
        grid=(nbatch, H_DIFF),
        in_specs=[pl.BlockSpec((1, seq, hd2), lambda b, h: (b, 0, h)),
                  pl.BlockSpec((1, seq, hd2), lambda b, h: (b, 0, h)),
                  pl.BlockSpec((1, ctx_len, hd2), lambda b, h: (b, 0, h)),
                  pl.BlockSpec((1, hd2, seq), lambda b, h: (b, h, 0)),
                  pl.BlockSpec((1, hd2, ctx_len), lambda b, h: (b, h, 0)),
                  _const_spec(lam_vec.shape), _const_spec((1, hd2))],
        out_specs=pl.BlockSpec((1, seq, hd2), lambda b, h: (b, 0, h)),
        scratch_shapes=[pltpu.VMEM((nkeys, hd2), BF16),
                        pltpu.VMEM((hd2 + V7X_BF16_SUBLANES, nkeys), BF16),
                        *[pltpu.VMEM((nkeys, 2 * Q_BLOCK), F32) for _ in range(3)]],
        compiler_params=_cparams("arbitrary", "arbitrary"),
        name="diff_attention",
    )(dq, dk, dkc, vt, vtc, lam_vec, subln_g)


WIN_GW = WIN_G * HEAD_DIM
WIN_REP_ROWS = 512
WIN_MROWS = 8


def _win_attn_kernel(q_ref, k_ref, v_ref, kc_ref, vc_ref, rep_ref, bias_ref, sink_ref, o_ref,
                     kr_ref, kcr_ref, vt_ref, vct_ref, *s_refs):
    seq = q_ref.shape[1]
    ctx_len = kc_ref.shape[1]
    tq = Q_BLOCK
    nq = seq // tq
    span = tq + 2 * WINDOW
    nkeys = ctx_len + span
    ncols = WIN_G * tq
    kvh = pl.program_id(1)
    rep = rep_ref[0]

    def widen(c, carry):
        r0 = pl.multiple_of(c * WIN_REP_ROWS, WIN_REP_ROWS)
        kr_ref[pl.ds(r0, WIN_REP_ROWS), :] = _dot(k_ref[0, pl.ds(r0, WIN_REP_ROWS), :], rep).astype(BF16)
        return carry

    lax.fori_loop(0, seq // WIN_REP_ROWS, widen, 0)
    kcr_ref[...] = _dot(kc_ref[0], rep).astype(BF16)
    vt_ref[:, 0:HEAD_DIM, :] = v_ref[0, :, 0]
    vt_ref[:, HEAD_DIM:, :] = jnp.ones((seq // V7X_LANES, V7X_BF16_SUBLANES, V7X_LANES), BF16)
    vct_ref[:, 0:HEAD_DIM, :] = vc_ref[0, :, 0]
    vct_ref[:, HEAD_DIM:, :] = jnp.ones((ctx_len // V7X_LANES, V7X_BF16_SUBLANES, V7X_LANES), BF16)
    lane_group = lax.broadcasted_iota(jnp.int32, (1, WIN_GW), 1) >> 6
    zero = jnp.zeros((), BF16)
    col_group = lax.broadcasted_iota(jnp.int32, (1, ncols), 1) >> 7
    sink = jnp.zeros((1, ncols), F32)
    for g in range(WIN_G):
        sink = jnp.where(col_group == g, sink_ref[kvh * WIN_G + g], sink)

    def first_window_chunk(t):
        return jnp.clip(t - WINDOW // tq, 0, nq - span // tq)

    def scores(t, s_ref):
        t = jnp.minimum(t, nq - 1)
        r0 = pl.multiple_of(t * tq, tq)
        w0 = pl.multiple_of(first_window_chunk(t) * tq, tq)
        edge = jnp.where(t == 0, 0, jnp.where(t == nq - 1, 2, 1))
        q = q_ref[0, pl.ds(r0, tq), :]
        q4 = jnp.concatenate([jnp.where(lane_group == g, q, zero) for g in range(WIN_G)], axis=0)
        s_c = _dot_nt(kcr_ref[...], q4)
        s_w = _dot_nt(kr_ref[pl.ds(w0, span), :], q4) + bias_ref[edge]
        s_ref[0:ctx_len] = s_c
        s_ref[ctx_len:nkeys] = s_w
        m = jnp.maximum(jnp.maximum(jnp.max(s_c, axis=0, keepdims=True),
                                    jnp.max(s_w, axis=0, keepdims=True)), sink)
        s_ref[nkeys:] = jnp.broadcast_to(m, (WIN_MROWS, ncols))

    def attend(t, s_ref):
        m = s_ref[nkeys:nkeys + 1]
        p = jnp.exp((s_ref[0:nkeys] - m).astype(BF16))
        c0 = first_window_chunk(t)
        v_c = jnp.concatenate([vct_ref[i] for i in range(ctx_len // V7X_LANES)], axis=1)
        v_w = jnp.concatenate([vt_ref[c0 + i] for i in range(span // V7X_LANES)], axis=1)
        acc = _dot(v_c, p[0:ctx_len]) + _dot(v_w, p[ctx_len:nkeys])
        o = acc[0:HEAD_DIM] / (acc[HEAD_DIM:HEAD_DIM + 1] + jnp.exp(sink - m))
        o4 = jnp.concatenate([o[:, g * tq:(g + 1) * tq] for g in range(WIN_G)], axis=0)
        o_ref[0, pl.ds(pl.multiple_of(t * tq, tq), tq), :] = o4.T.astype(BF16)

    def quad(i, carry):
        t = 4 * i
        scores(t + 2, s_refs[2])
        scores(t + 3, s_refs[3])
        attend(t, s_refs[0])
        attend(t + 1, s_refs[1])
        scores(t + 4, s_refs[0])
        scores(t + 5, s_refs[1])
        attend(t + 2, s_refs[2])
        attend(t + 3, s_refs[3])
        return carry

    scores(0, s_refs[0])
    scores(1, s_refs[1])
    lax.fori_loop(0, nq // 4, quad, 0)


def _win_attn(wq, wk, wv, wkc, wvc, rep, bias, sink):
    nbatch, seq, _ = wq.shape
    ctx_len = wkc.shape[1]
    assert (seq // Q_BLOCK) % 4 == 0
    span = Q_BLOCK + 2 * WINDOW
    srows, scols = ctx_len + span + WIN_MROWS, WIN_G * Q_BLOCK
    nch, ncc = seq // V7X_LANES, ctx_len // V7X_LANES
    wv = wv.reshape(nbatch, nch, H_WIN_KV, HEAD_DIM, V7X_LANES)
    wvc = wvc.reshape(nbatch, ncc, H_WIN_KV, HEAD_DIM, V7X_LANES)
    per_b = lambda b, j: (b, 0, 0)
    per_bh = lambda b, j: (b, 0, j)
    vrows = HEAD_DIM + V7X_BF16_SUBLANES
    return pl.pallas_call(
        _win_attn_kernel,
        out_shape=jax.ShapeDtypeStruct((nbatch, seq, WIN_Q), BF16),
        grid=(nbatch, H_WIN_KV),
        in_specs=[pl.BlockSpec((1, seq, WIN_GW), per_bh),
                  pl.BlockSpec((1, seq, WIN_KV), per_b),
                  pl.BlockSpec((1, nch, 1, HEAD_DIM, V7X_LANES), lambda b, j: (b, 0, j, 0, 0)),
                  pl.BlockSpec((1, ctx_len, WIN_KV), per_b),
                  pl.BlockSpec((1, ncc, 1, HEAD_DIM, V7X_LANES), lambda b, j: (b, 0, j, 0, 0)),
                  pl.BlockSpec((1, WIN_KV, WIN_GW), lambda b, j: (j, 0, 0)),
                  _const_spec(bias.shape),
                  pl.BlockSpec(memory_space=pltpu.SMEM)],
        out_specs=pl.BlockSpec((1, seq, WIN_GW), per_bh),
        scratch_shapes=[pltpu.VMEM((seq, WIN_GW), BF16), pltpu.VMEM((ctx_len, WIN_GW), BF16),
                        pltpu.VMEM((nch, vrows, V7X_LANES), BF16),
                        pltpu.VMEM((ncc, vrows, V7X_LANES), BF16),
                        *[pltpu.VMEM((srows, scols), F32) for _ in range(4)]],
        compiler_params=_cparams("arbitrary", "arbitrary"),
        name="window_attention",
    )(wq, wk, wv, wkc, wvc, rep, bias, sink)


def _cos_sin(num, den):
    ang = 2.0 * np.pi * (np.asarray(num, np.int64) % den).astype(np.float64) / den
    return np.cos(ang), np.sin(ang)


def _channel_dft_table():
    k = np.arange(FOURIER_GW)
    c, s = _cos_sin(np.outer(k, k), FOURIER_GW)
    return np.concatenate([c, -s], axis=1)


def _stage_a_table(n1c):
    k = np.arange(n1c)
    c, s = _cos_sin(np.outer(k, k), n1c)
    return np.kron(np.block([[c, s], [-s, c]]), np.eye(FFT_RB))


def _stage_b_table(norm):
    k = np.arange(V7X_LANES)
    c, s = _cos_sin(np.outer(k, k), V7X_LANES)
    kb = np.zeros((V7X_LANES, FFT_RB, 2, FFT_RB, V7X_LANES))
    for i in range(FFT_RB):
        kb[:, i, 0, i, :] = c / norm
        kb[:, i, 1, i, :] = s / norm
    return kb.reshape(V7X_LANES * FFT_RB, 2 * FFT_RB * V7X_LANES)


def _twiddle_tables(seq):
    n1c = seq // V7X_LANES
    c, s = _cos_sin(np.outer(np.arange(n1c), np.arange(V7X_LANES)), seq)
    expand = lambda t: np.repeat(t[:, :, None], V7X_LANES, axis=2)
    return expand(c), expand(s)


def _real_dft_table(n, norm):
    k = np.arange(n)
    c, s = _cos_sin(np.outer(k, k), n)
    return np.concatenate([c, s], axis=1) / norm


def _rope_tables(seq):
    t = np.arange(seq)
    axis_dim = HEAD_DIM // 2
    inv_freq = ROPE_BASE ** (-np.arange(0, axis_dim, 2, dtype=np.float64) / axis_dim)
    ang_r = (t // GRID_W).astype(np.float64)[:, None] * inv_freq[None, :]
    ang_c = (t % GRID_W).astype(np.float64)[:, None] * inv_freq[None, :]
    cos = np.concatenate([np.cos(ang_r)] * 2 + [np.cos(ang_c)] * 2, axis=1)
    sin = np.concatenate([-np.sin(ang_r), np.sin(ang_r), -np.sin(ang_c), np.sin(ang_c)], axis=1)
    reps = V7X_LANES // HEAD_DIM
    return np.tile(cos, (1, reps)), np.tile(sin, (1, reps))


def _window_bias_table():
    span = Q_BLOCK + 2 * WINDOW
    q = np.arange(Q_BLOCK)[None, :]
    k = np.arange(span)[:, None]
    out = []
    for start in (0, WINDOW, 2 * WINDOW):
        vis = np.abs(k - start - q) <= WINDOW
        out.append(np.tile(np.where(vis, 0.0, NEG_INF), (1, WIN_G)))
    return np.stack(out).astype(np.float32)


def _replication_table():
    rep = np.zeros((H_WIN_KV, WIN_KV, WIN_GW), np.float32)
    for j in range(H_WIN_KV):
        for g in range(WIN_G):
            for d in range(HEAD_DIM):
                rep[j, j * HEAD_DIM + d, g * HEAD_DIM + d] = 1.0
    return rep


def _fourier_conv_mix(h2d, mod, gpre, w_in, conv_w, conv_b, ln_g, ln_b, nbatch, tm):
    t = h2d.shape[0]
    seq = t // nbatch
    table = lambda a: jnp.asarray(a, F32).astype(BF16)
    cw = table(_channel_dft_table())
    norm = math.sqrt(seq * FOURIER_GW)
    if seq % (FFT_RB * V7X_LANES) == 0:
        n1c = seq // V7X_LANES
        twc, tws = _twiddle_tables(seq)
        z, u = _ev_fft_a(h2d, mod, gpre, w_in, cw, table(_stage_a_table(n1c)),
                         jnp.asarray(twc, F32), jnp.asarray(tws, F32), nbatch)
        fa = _ev_fft_b(z, table(_stage_b_table(norm)))
        u = u.reshape(t, D_CONV)
    else:
        p, u = _ev_proj(h2d, mod, gpre, w_in, cw, nbatch, tm)
        fa = _dft_real(p.reshape(nbatch, 2 * seq, D_FOURIER), table(_real_dft_table(seq, norm)),
                       D_FOURIER)
    fa = fa.reshape(t, D_FOURIER)
    uc = _conv_ln_swish(u, conv_w, conv_b, ln_g, ln_b, nbatch, tm)
    return fa, uc


def _diff_window_mix(x2d, c2d, modx, modc, gpre, w_in, lam_vec, subln_g, sink, lam_init, nbatch):
    t = x2d.shape[0]
    seq = t // nbatch
    ctx_len = c2d.shape[0] // nbatch
    cos, sin = _rope_tables(seq)
    dq, dk, vt, wq, wk, wv = _od_proj(x2d, modx, gpre, w_in, jnp.asarray(cos, F32),
                                      jnp.asarray(sin, F32), nbatch, KV_CHUNK)
    dkc, vtc, wkc, wvc = _od_proj_ctx(c2d, modc, gpre, w_in, nbatch)
    b3 = lambda a, n: a.reshape(nbatch, n, a.shape[-1])
    diff = _diff_attn(b3(dq, seq), b3(dk, seq), vt, b3(dkc, ctx_len), vtc, lam_vec, subln_g, lam_init)
    win = _win_attn(b3(wq, seq), b3(wk, seq), wv, b3(wkc, ctx_len), wvc,
                    jnp.asarray(_replication_table(), F32).astype(BF16),
                    jnp.asarray(_window_bias_table(), F32), sink)
    return diff.reshape(t, DIFF_V), win.reshape(t, WIN_Q)


def kernel(x, c, ctx, c_ctx, w_mod, b_mod, norm_pre, norm_post, ffn_w_in, ffn_w_out, ev_w_in, ev_conv_w, ev_conv_b, ev_ln_g, ev_ln_b, ev_w_out, od_w_in, od_lambda, od_subln_g, od_sink, od_w_out):
    nbatch, seq, d = x.shape
    ctx_len = ctx.shape[1]
    depth = w_mod.shape[0]
    assert d == D_MODEL and depth == DEPTH == 2 and ctx_len == KV_CHUNK
    assert seq % FFN_TM == 0 and seq % (8 * V7X_LANES) == 0

    mod_rows = 8
    cs = jnp.concatenate([c, c_ctx[None, :], jnp.zeros((mod_rows - nbatch - 1, d), F32)], axis=0)
    mod = _modulation(cs, w_mod, b_mod).reshape(depth, mod_rows, N_MOD, d)
    row = lambda v: v.reshape(1, -1)

    x2d = x.reshape(nbatch * seq, d)
    c2d = ctx.reshape(nbatch * ctx_len, d)
    ffn_in = ffn_w_in.astype(BF16)
    ffn_out = ffn_w_out.astype(BF16)

    def ffn_x(x2d, l, s, k):
        return _ffn(x2d, mod[l, :nbatch], row(norm_pre[l, s]), row(norm_post[l, s]),
                    ffn_in, ffn_out, (l, k), s, FFN_TM, seq // FFN_TM)

    def ffn_c(c2d, l, s, k):
        return _ffn(c2d, mod[l, nbatch:nbatch + 1], row(norm_pre[l, s]), row(norm_post[l, s]),
                    ffn_in, ffn_out, (l, k), s, ctx_len, 1)

    def mix_ffn(branches, w_mix, h2d, l, m, tm, tpb):
        return _mix_ffn(*branches, w_mix, row(norm_post[l, 1]), h2d, m, row(norm_pre[l, 2]),
                        row(norm_post[l, 2]), ffn_in, ffn_out, (l, 1), 2, tm, tpb)

    l = 0
    modx, modc = mod[l, :nbatch], mod[l, nbatch:nbatch + 1]
    x2d = ffn_x(x2d, l, 0, 0)
    c2d = ffn_c(c2d, l, 0, 0)
    ev_args = (row(norm_pre[l, 1]), ev_w_in[0].astype(BF16), ev_conv_w[0],
               row(ev_conv_b[0]), row(ev_ln_g[0]), row(ev_ln_b[0]))
    ev_out = ev_w_out[0].astype(BF16)
    x2d = mix_ffn(_fourier_conv_mix(x2d, modx, *ev_args, nbatch, ROW_TM), ev_out, x2d, l, modx,
                  FFN_TM, seq // FFN_TM)
    c2d = mix_ffn(_fourier_conv_mix(c2d, modc, *ev_args, nbatch, ctx_len), ev_out, c2d, l, modc,
                  ctx_len, 1)

    l = 1
    modx, modc = mod[l, :nbatch], mod[l, nbatch:nbatch + 1]
    x2d = ffn_x(x2d, l, 0, 0)
    c2d = ffn_c(c2d, l, 0, 0)
    lam_init = 0.8 - 0.6 * math.exp(-0.3 * l)
    branches = _diff_window_mix(x2d, c2d, modx, modc, row(norm_pre[l, 1]), od_w_in[0].astype(BF16),
                                od_lambda[0], row(od_subln_g[0]), od_sink[0], lam_init, nbatch)
    x2d = mix_ffn(branches, od_w_out[0].astype(BF16), x2d, l, modx, FFN_TM, seq // FFN_TM)
    return x2d.reshape(nbatch, seq, d)
```

```python
import functools
import math

import numpy as np
import jax
import jax.numpy as jnp
from jax import lax
from jax.experimental import pallas as pl
from jax.experimental.pallas import tpu as pltpu

F32 = jnp.float32
BF16 = jnp.bfloat16

D_MODEL = 1024
DEPTH = 2
GRID_W = 64
N_MOD = 9
EPS = 1e-6
NEG_INF = -1e30
HALF_STEP = 0.5
D_FF = 2816
FOURIER_GROUPS = 4
FOURIER_GW = D_MODEL // 8
D_FOURIER = FOURIER_GROUPS * FOURIER_GW
D_CONV = D_MODEL // 2
CONV_W = 31
HEAD_DIM = 64
ROPE_BASE = 10000.0
H_DIFF = 4
DIFF_QK = H_DIFF * 2 * HEAD_DIM
DIFF_V = H_DIFF * 2 * HEAD_DIM
H_WIN = 8
H_WIN_KV = 2
WIN_G = H_WIN // H_WIN_KV
WIN_Q = H_WIN * HEAD_DIM
WIN_KV = H_WIN_KV * HEAD_DIM
WINDOW = 128
Q_BLOCK = 128

V7X_LANES = 128
V7X_MXU_DIM = 256
V7X_BF16_SUBLANES = 16
V7X_VMEM_LIMIT_BYTES = 56 * 1024 * 1024

FFN_TM = 1024
FFN_SUB = 512
FFN_FC = V7X_MXU_DIM
ROW_TM = 512
LOG2_E = math.log2(math.e)
KV_CHUNK = 256
FFT_RB = 8
CONV_RB = 32
CONV_HALO = V7X_BF16_SUBLANES
CONV_BLK = 128
CONV_SPAN = CONV_BLK + 2 * CONV_HALO
CONV_K = V7X_MXU_DIM


def _cparams(*sem):
    return pltpu.CompilerParams(dimension_semantics=sem, vmem_limit_bytes=V7X_VMEM_LIMIT_BYTES)


def _const_spec(shape):
    n = len(shape)
    return pl.BlockSpec(shape, lambda *_: (0,) * n, pipeline_mode=pl.Buffered(1))


def _dot(a, b):
    return jnp.dot(a, b, preferred_element_type=F32)


def _dot_nt(a, b):
    return lax.dot_general(a, b, (((1,), (1,)), ((), ())), preferred_element_type=F32)


def _rms(xf, g):
    return xf * lax.rsqrt(jnp.mean(xf * xf, axis=-1, keepdims=True) + EPS) * g


def _mrow(m_ref, j):
    return m_ref[0, j:j + 1, :]


def _pre(x, g, m_ref, s):
    return _rms(x, g) * (1.0 + _mrow(m_ref, 3 * s + 1)) + _mrow(m_ref, 3 * s)


def _mod_spec(mod, tiles_per_batch):
    d = mod.shape[-1]
    if mod.shape[0] == 1:
        return pl.BlockSpec((1, N_MOD, d), lambda i: (0, 0, 0))
    return pl.BlockSpec((1, N_MOD, d), lambda i: (i // tiles_per_batch, 0, 0))


def _sigmoid(x):
    return 1.0 / (1.0 + jnp.exp(-x))


def _mod_kernel(c_ref, w_ref, b_ref, o_ref):
    c = c_ref[...]
    h = c * _sigmoid(c)
    o_ref[0] = jnp.dot(h, w_ref[0], preferred_element_type=F32,
                       precision=lax.Precision.HIGHEST) + b_ref[0]


def _modulation(cs, w_mod, b_mod):
    depth, d, nd = w_mod.shape
    rows = cs.shape[0]
    return pl.pallas_call(
        _mod_kernel,
        out_shape=jax.ShapeDtypeStruct((depth, rows, nd), F32),
        grid=(depth, nd // d),
        in_specs=[pl.BlockSpec((rows, d), lambda l, j: (0, 0)),
                  pl.BlockSpec((1, d, d), lambda l, j: (l, 0, j)),
                  pl.BlockSpec((1, 1, d), lambda l, j: (l, 0, j))],
        out_specs=pl.BlockSpec((1, rows, d), lambda l, j: (l, 0, j)),
        compiler_params=_cparams("arbitrary", "arbitrary"),
        name="modulation",
    )(cs, w_mod, b_mod.reshape(depth, 1, nd))


def _swiglu_rows(s, x, rows, m_ref, gpre_ref, gpost_ref, win_ref, wout_ref, o_ref, a_ref):
    h = _pre(x, gpre_ref[...], m_ref, s).astype(BF16)
    for c in range(D_FF // FFN_FC):
        lo = c * FFN_FC
        g = _dot(h, win_ref[:, lo:lo + FFN_FC])
        u = _dot(h, win_ref[:, D_FF + lo:D_FF + lo + FFN_FC])
        a_ref[rows, lo:lo + FFN_FC] = (g * _sigmoid(g) * u).astype(BF16)
    y = _dot(a_ref[rows, :], wout_ref[...])
    o_ref[rows, :] = x + HALF_STEP * _mrow(m_ref, 3 * s + 2) * _rms(y, gpost_ref[...])


def _row_subtiles(n):
    sub = min(FFN_SUB, n)
    return [slice(j * sub, (j + 1) * sub) for j in range(n // sub)]


def _ffn_kernel(s, x_ref, m_ref, gpre_ref, gpost_ref, win_ref, wout_ref, o_ref, a_ref):
    for rows in _row_subtiles(x_ref.shape[0]):
        _swiglu_rows(s, x_ref[rows, :], rows, m_ref, gpre_ref, gpost_ref, win_ref, wout_ref, o_ref, a_ref)


def _mix_ffn_kernel(s, ma_ref, mb_ref, wmix_ref, gmix_ref, x_ref, m_ref, gpre_ref, gpost_ref,
                    win_ref, wout_ref, o_ref, a_ref):
    ka = ma_ref.shape[1]
    for rows in _row_subtiles(x_ref.shape[0]):
        y = (_dot(ma_ref[rows, :].astype(BF16), wmix_ref[:ka])
             + _dot(mb_ref[rows, :].astype(BF16), wmix_ref[ka:]))
        x = x_ref[rows, :] + _mrow(m_ref, 5) * _rms(y, gmix_ref[...])
        _swiglu_rows(s, x, rows, m_ref, gpre_ref, gpost_ref, win_ref, wout_ref, o_ref, a_ref)


def _stacked_weight_spec(w, lk):
    l, k = lk
    return pl.BlockSpec((None, None) + w.shape[2:], lambda i: (l, k, 0, 0),
                        pipeline_mode=pl.Buffered(1))


def _ffn(x2d, mod, gpre, gpost, w_in, w_out, lk, s, tm, tiles_per_batch):
    t, d = x2d.shape
    return pl.pallas_call(
        functools.partial(_ffn_kernel, s),
        out_shape=jax.ShapeDtypeStruct((t, d), F32),
        grid=(t // tm,),
        in_specs=[pl.BlockSpec((tm, d), lambda i: (i, 0)),
                  _mod_spec(mod, tiles_per_batch),
                  _const_spec((1, d)), _const_spec((1, d)),
                  _stacked_weight_spec(w_in, lk), _stacked_weight_spec(w_out, lk)],
        out_specs=pl.BlockSpec((tm, d), lambda i: (i, 0)),
        scratch_shapes=[pltpu.VMEM((tm, D_FF), BF16)],
        compiler_params=_cparams("arbitrary"),
        name="swiglu_halfstep",
    )(x2d, mod, gpre, gpost, w_in, w_out)


def _mix_ffn(ma, mb, w_mix, g_mix, x2d, mod, gpre, gpost, w_in, w_out, lk, s, tm, tiles_per_batch):
    t, d = x2d.shape
    row = lambda i: (i, 0)
    return pl.pallas_call(
        functools.partial(_mix_ffn_kernel, s),
        out_shape=jax.ShapeDtypeStruct((t, d), F32),
        grid=(t // tm,),
        in_specs=[pl.BlockSpec((tm, ma.shape[1]), row), pl.BlockSpec((tm, mb.shape[1]), row),
                  _const_spec(w_mix.shape), _const_spec((1, d)),
                  pl.BlockSpec((tm, d), row),
                  _mod_spec(mod, tiles_per_batch),
                  _const_spec((1, d)), _const_spec((1, d)),
                  _stacked_weight_spec(w_in, lk), _stacked_weight_spec(w_out, lk)],
        out_specs=pl.BlockSpec((tm, d), row),
        scratch_shapes=[pltpu.VMEM((tm, D_FF), BF16)],
        compiler_params=_cparams("arbitrary"),
        name="outproj_swiglu_halfstep",
    )(ma, mb, w_mix, g_mix, x2d, mod, gpre, gpost, w_in, w_out)


def _ev_proj_kernel(x_ref, m_ref, gpre_ref, w_ref, cw_ref, p_ref, u_ref):
    h = _pre(x_ref[...], gpre_ref[...], m_ref, 1).astype(BF16)
    a = _dot(h, w_ref[:, :D_FOURIER])
    g1 = _dot(h, w_ref[:, D_FOURIER:D_FOURIER + D_CONV])
    g2 = _dot(h, w_ref[:, D_FOURIER + D_CONV:])
    u_ref[...] = (g1 * _sigmoid(g2)).astype(BF16)
    gw = FOURIER_GW
    for g in range(FOURIER_GROUPS):
        pg = _dot(a[:, g * gw:(g + 1) * gw].astype(BF16), cw_ref[...])
        p_ref[0, 0, :, g * gw:(g + 1) * gw] = pg[:, :gw].astype(BF16)
        p_ref[0, 1, :, g * gw:(g + 1) * gw] = pg[:, gw:].astype(BF16)


def _ev_proj(x2d, mod, gpre, w_in, cw, nbatch, tm):
    t, d = x2d.shape
    seq = t // nbatch
    tpb = seq // tm
    return pl.pallas_call(
        _ev_proj_kernel,
        out_shape=(jax.ShapeDtypeStruct((nbatch, 2, seq, D_FOURIER), BF16),
                   jax.ShapeDtypeStruct((t, D_CONV), BF16)),
        grid=(t // tm,),
        in_specs=[pl.BlockSpec((tm, d), lambda i: (i, 0)),
                  _mod_spec(mod, tpb),
                  _const_spec((1, d)), _const_spec(w_in.shape), _const_spec(cw.shape)],
        out_specs=(pl.BlockSpec((1, 2, tm, D_FOURIER), lambda i: (i // tpb, 0, i % tpb, 0)),
                   pl.BlockSpec((tm, D_CONV), lambda i: (i, 0))),
        compiler_params=_cparams("arbitrary"),
        name="even_proj",
    )(x2d, mod, gpre, w_in, cw)


def _ev_fft_a_kernel(x_ref, m_ref, gpre_ref, w_ref, cw_ref, ka_ref, twc_ref, tws_ref, z_ref, u_ref):
    n1c = x_ref.shape[1]
    rows = n1c * FFT_RB
    x = x_ref[0].reshape(rows, D_MODEL)
    h = _pre(x, gpre_ref[...], m_ref, 1).astype(BF16)
    a = _dot(h, w_ref[:, :D_FOURIER])
    g1 = _dot(h, w_ref[:, D_FOURIER:D_FOURIER + D_CONV])
    g2 = _dot(h, w_ref[:, D_FOURIER + D_CONV:])
    u_ref[0] = (g1 * _sigmoid(g2)).reshape(n1c, FFT_RB, D_CONV)
    gw = FOURIER_GW
    pr, pi = [], []
    for g in range(FOURIER_GROUPS):
        pg = _dot(a[:, g * gw:(g + 1) * gw].astype(BF16), cw_ref[...])
        pr.append(pg[:, :gw])
        pi.append(pg[:, gw:])
    p = jnp.concatenate([jnp.concatenate(pr, axis=1), jnp.concatenate(pi, axis=1)], axis=0)
    z = _dot(ka_ref[...], p.astype(BF16))
    zr, zi = z[:rows], z[rows:]
    tc = jnp.concatenate([twc_ref[...].reshape(rows, V7X_LANES)] * FOURIER_GROUPS, axis=1)
    ts = jnp.concatenate([tws_ref[...].reshape(rows, V7X_LANES)] * FOURIER_GROUPS, axis=1)
    z_ref[0, 0] = (zr * tc + zi * ts).reshape(n1c, FFT_RB, D_FOURIER)
    z_ref[0, 1] = (zi * tc - zr * ts).reshape(n1c, FFT_RB, D_FOURIER)


def _ev_fft_a(x2d, mod, gpre, w_in, cw, ka, twc, tws, nbatch):
    t, d = x2d.shape
    seq = t // nbatch
    n1c = seq // V7X_LANES
    nrb = V7X_LANES // FFT_RB
    x4 = x2d.reshape(nbatch, n1c, V7X_LANES, d)
    blk = lambda w: pl.BlockSpec((1, n1c, FFT_RB, w), lambda i: (i // nrb, 0, i % nrb, 0))
    return pl.pallas_call(
        _ev_fft_a_kernel,
        out_shape=(jax.ShapeDtypeStruct((nbatch, 2, n1c, V7X_LANES, D_FOURIER), F32),
                   jax.ShapeDtypeStruct((nbatch, n1c, V7X_LANES, D_CONV), F32)),
        grid=(nbatch * nrb,),
        in_specs=[blk(d), _mod_spec(mod, nrb), _const_spec((1, d)), _const_spec(w_in.shape),
                  _const_spec(cw.shape), _const_spec(ka.shape),
                  pl.BlockSpec((n1c, FFT_RB, V7X_LANES), lambda i: (0, i % nrb, 0)),
                  pl.BlockSpec((n1c, FFT_RB, V7X_LANES), lambda i: (0, i % nrb, 0))],
        out_specs=(pl.BlockSpec((1, 2, n1c, FFT_RB, D_FOURIER), lambda i: (i // nrb, 0, 0, i % nrb, 0)),
                   blk(D_CONV)),
        compiler_params=_cparams("arbitrary"),
        name="even_proj_fft_a",
    )(x4, mod, gpre, w_in, cw, ka, twc, tws)


def _ev_fft_b_kernel(kb_ref, z_ref, o_ref):
    z = z_ref[0].reshape(2 * FFT_RB * V7X_LANES, D_FOURIER).astype(BF16)
    o_ref[0] = _dot(kb_ref[...], z).reshape(V7X_LANES, FFT_RB, D_FOURIER)


def _ev_fft_b(z, kb):
    nbatch, _, n1c, _, nc = z.shape
    return pl.pallas_call(
        _ev_fft_b_kernel,
        out_shape=jax.ShapeDtypeStruct((nbatch, V7X_LANES, n1c, nc), F32),
        grid=(nbatch, n1c // FFT_RB),
        in_specs=[_const_spec(kb.shape),
                  pl.BlockSpec((1, 2, FFT_RB, V7X_LANES, nc), lambda b, j: (b, 0, j, 0, 0))],
        out_specs=pl.BlockSpec((1, V7X_LANES, FFT_RB, nc), lambda b, j: (b, 0, j, 0)),
        compiler_params=_cparams("arbitrary", "arbitrary"),
        name="fft_b",
    )(kb, z)


def _dft_real_kernel(mat_ref, z_ref, o_ref):
    o_ref[0] = _dot(mat_ref[...], z_ref[0]).astype(o_ref.dtype)


def _dft_real(z, mat, nb):
    nbatch, k, n = z.shape
    m = mat.shape[0]
    return pl.pallas_call(
        _dft_real_kernel,
        out_shape=jax.ShapeDtypeStruct((nbatch, m, n), BF16),
        grid=(nbatch, n // nb),
        in_specs=[_const_spec(mat.shape), pl.BlockSpec((1, k, nb), lambda b, j: (b, 0, j))],
        out_specs=pl.BlockSpec((1, m, nb), lambda b, j: (b, 0, j)),
        compiler_params=_cparams("arbitrary", "arbitrary"),
        name="dft_real",
    )(mat, z)


def _conv_kernel(tpb, u_ref, up_ref, un_ref, sh_ref, cw_ref, cb_ref, lg_ref, lb_ref, o_ref,
                 ext_ref, cp_ref):
    tm = u_ref.shape[0]
    j = pl.program_id(0) % tpb
    halo = CONV_HALO
    ext_ref[0:halo] = jnp.where(j > 0, up_ref[...], 0).astype(BF16)
    ext_ref[halo:halo + tm] = u_ref[...].astype(BF16)
    ext_ref[halo + tm:tm + 2 * halo] = jnp.where(j < tpb - 1, un_ref[...], 0).astype(BF16)
    ext_ref[tm + 2 * halo:] = jnp.zeros((CONV_K - CONV_SPAN, D_CONV), BF16)
    cb, lg, lb = cb_ref[...], lg_ref[...], lb_ref[...]
    for blk in range(tm // CONV_BLK):
        b0 = blk * CONV_BLK
        cp_ref[blk] = _dot(sh_ref[...], ext_ref[b0:b0 + CONV_K, :])
        for c in range(CONV_BLK // CONV_RB):
            base = c * CONV_RB
            acc = None
            for t in range(CONV_W):
                r, off = t % 8, t - t % 8
                rows = cp_ref[blk, r * CONV_SPAN + base + off:r * CONV_SPAN + base + off + CONV_RB, :]
                term = cw_ref[t:t + 1, :] * rows
                acc = term if acc is None else acc + term
            v = acc + cb
            vc = v - jnp.mean(v, axis=-1, keepdims=True)
            var = jnp.mean(vc * vc, axis=-1, keepdims=True)
            y = vc * lax.rsqrt(var + EPS) * lg + lb
            o_ref[b0 + base:b0 + base + CONV_RB, :] = (y * _sigmoid(y)).astype(BF16)


def _conv_shift_table():
    first = CONV_HALO - CONV_W // 2
    sh = np.zeros((8, CONV_SPAN, CONV_K), np.float32)
    for r in range(8):
        for m in range(CONV_SPAN - first - r):
            sh[r, m, m + first + r] = 1.0
    return sh.reshape(8 * CONV_SPAN, CONV_K)


def _conv_ln_swish(u, conv_w, conv_b, ln_g, ln_b, nbatch, tm):
    t, dc = u.shape
    tpb = (t // nbatch) // tm
    hb = tm // CONV_HALO
    last = t // CONV_HALO - 1
    shift = jnp.asarray(_conv_shift_table(), F32).astype(BF16)
    return pl.pallas_call(
        functools.partial(_conv_kernel, tpb),
        out_shape=jax.ShapeDtypeStruct((t, dc), BF16),
        grid=(t // tm,),
        in_specs=[pl.BlockSpec((tm, dc), lambda i: (i, 0)),
                  pl.BlockSpec((CONV_HALO, dc), lambda i: (jnp.maximum(i * hb - 1, 0), 0)),
                  pl.BlockSpec((CONV_HALO, dc), lambda i: (jnp.minimum((i + 1) * hb, last), 0)),
                  _const_spec(shift.shape),
                  _const_spec(conv_w.shape), _const_spec((1, dc)), _const_spec((1, dc)),
                  _const_spec((1, dc))],
        out_specs=pl.BlockSpec((tm, dc), lambda i: (i, 0)),
        scratch_shapes=[pltpu.VMEM((tm + CONV_K - CONV_BLK, dc), BF16),
                        pltpu.VMEM((tm // CONV_BLK, 8 * CONV_SPAN, dc), F32)],
        compiler_params=_cparams("arbitrary"),
        name="conv_ln_swish",
    )(u, u, u, shift, conv_w, conv_b, ln_g, ln_b)


def _rope(z, cos, sin, even_block):
    out = []
    for j in range(z.shape[1] // V7X_LANES):
        zs = z[:, j * V7X_LANES:(j + 1) * V7X_LANES]
        sw = jnp.where(even_block, pltpu.roll(zs, V7X_LANES - 16, 1), pltpu.roll(zs, 16, 1))
        out.append(zs * cos + sw * sin)
    return out[0] if len(out) == 1 else jnp.concatenate(out, axis=1)


def _store_transposed_chunks(o_ref, z):
    zt = z.T.astype(BF16)
    for c in range(z.shape[0] // V7X_LANES):
        o_ref[0, c] = zt[:, c * V7X_LANES:(c + 1) * V7X_LANES]


def _od_proj_kernel(x_ref, m_ref, gpre_ref, w_ref, cos_ref, sin_ref,
                    dq_ref, dk_ref, vt_ref, wq_ref, wk_ref, wv_ref):
    h = _pre(x_ref[...], gpre_ref[...], m_ref, 1).astype(BF16)
    cos, sin = cos_ref[...], sin_ref[...]
    lane = lax.broadcasted_iota(jnp.int32, (1, V7X_LANES), 1)
    even_block = ((lane >> 4) & 1) == 0
    scale = HEAD_DIM ** -0.5
    o = 0
    dq_ref[...] = (_rope(_dot(h, w_ref[:, o:o + DIFF_QK]), cos, sin, even_block)
                   * (scale * LOG2_E)).astype(BF16)
    o += DIFF_QK
    dk_ref[...] = _rope(_dot(h, w_ref[:, o:o + DIFF_QK]), cos, sin, even_block).astype(BF16)
    o += DIFF_QK
    vt_ref[0] = _dot(h, w_ref[:, o:o + DIFF_V]).T.astype(BF16)
    o += DIFF_V
    wq_ref[...] = (_rope(_dot(h, w_ref[:, o:o + WIN_Q]), cos, sin, even_block) * scale).astype(BF16)
    o += WIN_Q
    wk_ref[...] = _rope(_dot(h, w_ref[:, o:o + WIN_KV]), cos, sin, even_block).astype(BF16)
    o += WIN_KV
    _store_transposed_chunks(wv_ref, _dot(h, w_ref[:, o:o + WIN_KV]))


def _od_proj(x2d, mod, gpre, w_in, cos, sin, nbatch, tm):
    t, d = x2d.shape
    seq = t // nbatch
    tpb = seq // tm
    row = lambda i: (i, 0)
    return pl.pallas_call(
        _od_proj_kernel,
        out_shape=(jax.ShapeDtypeStruct((t, DIFF_QK), BF16),
                   jax.ShapeDtypeStruct((t, DIFF_QK), BF16),
                   jax.ShapeDtypeStruct((nbatch, DIFF_V, seq), BF16),
                   jax.ShapeDtypeStruct((t, WIN_Q), BF16),
                   jax.ShapeDtypeStruct((t, WIN_KV), BF16),
                   jax.ShapeDtypeStruct((nbatch, seq // V7X_LANES, WIN_KV, V7X_LANES), BF16)),
        grid=(t // tm,),
        in_specs=[pl.BlockSpec((tm, d), row),
                  _mod_spec(mod, tpb),
                  _const_spec((1, d)), _const_spec(w_in.shape),
                  pl.BlockSpec((tm, V7X_LANES), lambda i: (i % tpb, 0)),
                  pl.BlockSpec((tm, V7X_LANES), lambda i: (i % tpb, 0))],
        out_specs=(pl.BlockSpec((tm, DIFF_QK), row),
                   pl.BlockSpec((tm, DIFF_QK), row),
                   pl.BlockSpec((1, DIFF_V, tm), lambda i: (i // tpb, 0, i % tpb)),
                   pl.BlockSpec((tm, WIN_Q), row),
                   pl.BlockSpec((tm, WIN_KV), row),
                   pl.BlockSpec((1, tm // V7X_LANES, WIN_KV, V7X_LANES),
                                lambda i: (i // tpb, i % tpb, 0, 0))),
        compiler_params=_cparams("arbitrary"),
        name="odd_proj",
    )(x2d, mod, gpre, w_in, cos, sin)


def _od_proj_ctx_kernel(x_ref, m_ref, gpre_ref, w_ref, dk_ref, vt_ref, wk_ref, wv_ref):
    h = _pre(x_ref[...], gpre_ref[...], m_ref, 1).astype(BF16)
    o = DIFF_QK
    dk_ref[...] = _dot(h, w_ref[:, o:o + DIFF_QK]).astype(BF16)
    o += DIFF_QK
    vt_ref[0] = _dot(h, w_ref[:, o:o + DIFF_V]).T.astype(BF16)
    o += DIFF_V + WIN_Q
    wk_ref[...] = _dot(h, w_ref[:, o:o + WIN_KV]).astype(BF16)
    o += WIN_KV
    _store_transposed_chunks(wv_ref, _dot(h, w_ref[:, o:o + WIN_KV]))


def _od_proj_ctx(c2d, mod, gpre, w_in, nbatch):
    t, d = c2d.shape
    tm = t // nbatch
    row = lambda i: (i, 0)
    return pl.pallas_call(
        _od_proj_ctx_kernel,
        out_shape=(jax.ShapeDtypeStruct((t, DIFF_QK), BF16),
                   jax.ShapeDtypeStruct((nbatch, DIFF_V, tm), BF16),
                   jax.ShapeDtypeStruct((t, WIN_KV), BF16),
                   jax.ShapeDtypeStruct((nbatch, tm // V7X_LANES, WIN_KV, V7X_LANES), BF16)),
        grid=(nbatch,),
        in_specs=[pl.BlockSpec((tm, d), row), _const_spec(mod.shape),
                  _const_spec((1, d)), _const_spec(w_in.shape)],
        out_specs=(pl.BlockSpec((tm, DIFF_QK), row),
                   pl.BlockSpec((1, DIFF_V, tm), lambda i: (i, 0, 0)),
                   pl.BlockSpec((tm, WIN_KV), row),
                   pl.BlockSpec((1, tm // V7X_LANES, WIN_KV, V7X_LANES), lambda i: (i, 0, 0, 0))),
        compiler_params=_cparams("arbitrary"),
        name="odd_proj_ctx",
    )(c2d, mod, gpre, w_in)


def _diff_attn_kernel(lam_init, q_ref, kx_ref, kc_ref, vx_ref, vc_ref, lam_ref, g_ref, o_ref,
                      k_all, v_all, sa_ref, sb_ref, sc_ref):
    tq = Q_BLOCK
    seq = q_ref.shape[1]
    ctx_len = kc_ref.shape[1]
    nkeys = ctx_len + seq
    nq = seq // tq
    hd2 = 2 * HEAD_DIM
    k_all[0:ctx_len] = kc_ref[0]
    k_all[ctx_len:nkeys] = kx_ref[0]
    v_all[0:hd2, 0:ctx_len] = vc_ref[0]
    v_all[0:hd2, ctx_len:nkeys] = vx_ref[0]
    v_all[hd2:, :] = jnp.ones((V7X_BF16_SUBLANES, nkeys), BF16)
    lv = lam_ref[...]
    lam = (jnp.exp(jnp.sum(lv[0:1] * lv[1:2], axis=-1, keepdims=True))
           - jnp.exp(jnp.sum(lv[2:3] * lv[3:4], axis=-1, keepdims=True)) + lam_init)
    lane = lax.broadcasted_iota(jnp.int32, (1, hd2), 1)
    first_half = lane < HEAD_DIM
    zero = jnp.zeros((), BF16)

    def scores(t, s_ref):
        r0 = pl.multiple_of(t * tq, tq)
        q = q_ref[0, pl.ds(r0, tq), :]
        qq = jnp.concatenate([jnp.where(first_half, q, zero),
                              jnp.where(first_half, zero, q)], axis=0)
        s = _dot_nt(k_all[...], qq)
        s_ref[...] = s
        return jnp.max(s, axis=0, keepdims=True)

    def attend(t, s_ref, mcol):
        p = jnp.exp2((s_ref[...] - mcol).astype(BF16))
        acc = _dot(v_all[...], p)
        o = acc[:hd2] / acc[hd2:hd2 + 1]
        d = (o[:, :tq] - lam * o[:, tq:]).T
        y = _rms(d, g_ref[...]) * (1.0 - lam_init)
        o_ref[0, pl.ds(pl.multiple_of(t * tq, tq), tq), :] = y.astype(BF16)

    def triple(i, carry):
        m_a, m_b = carry
        t = 3 * i
        m_c = scores(t + 2, sc_ref)
        attend(t, sa_ref, m_a)
        m_a = scores(t + 3, sa_ref)
        attend(t + 1, sb_ref, m_b)
        m_b = scores(jnp.minimum(t + 4, nq - 1), sb_ref)
        attend(t + 2, sc_ref, m_c)
        return m_a, m_b

    m_a, _ = lax.fori_loop(0, (nq - 1) // 3, triple, (scores(0, sa_ref), scores(1, sb_ref)))
    attend(nq - 1, sa_ref, m_a)


def _diff_attn(dq, dk, vt, dkc, vtc, lam_vec, subln_g, lam_init):
    nbatch, seq, _ = dq.shape
    ctx_len = dkc.shape[1]
    nkeys = ctx_len + seq
    hd2 = 2 * HEAD_DIM
    assert (seq // Q_BLOCK - 1) % 3 == 0
    return pl.pallas_call(
        functools.partial(_diff_attn_kernel, lam_init),
        out_shape=jax.ShapeDtypeStruct((nbatch, seq, DIFF_V), BF16),
        grid=(nbatch, H_DIFF),
        in_specs=[pl.BlockSpec((1, seq, hd2), lambda b, h: (b, 0, h)),
                  pl.BlockSpec((1, seq, hd2), lambda b, h: (b, 0, h)),
                  pl.BlockSpec((1, ctx_len, hd2), lambda b, h: (b, 0, h)),
                  pl.BlockSpec((1, hd2, seq), lambda b, h: (b, h, 0)),
                  pl.BlockSpec((1, hd2, ctx_len), lambda b, h: (b, h, 0)),
                  _const_spec(lam_vec.shape), _const_spec((1, hd2))],
        out_specs=pl.BlockSpec((1, seq, hd2), lambda b, h: (b, 0, h)),
        scratch_shapes=[pltpu.VMEM((nkeys, hd2), BF16),
                        pltpu.VMEM((hd2 + V7X_BF16_SUBLANES, nkeys), BF16),
                        *[pltpu.VMEM((nkeys, 2 * Q_BLOCK), F32) for _ in range(3)]],
        compiler_params=_cparams("arbitrary", "arbitrary"),
        name="diff_attention",
    )(dq, dk, dkc, vt, vtc, lam_vec, subln_g)


WIN_GW = WIN_G * HEAD_DIM
WIN_REP_ROWS = 512
WIN_MROWS = 8


def _win_attn_kernel(q_ref, k_ref, v_ref, kc_ref, vc_ref, rep_ref, bias_ref, sink_ref, o_ref,
                     kr_ref, kcr_ref, vt_ref, vct_ref, *s_refs):
    seq = q_ref.shape[1]
    ctx_len = kc_ref.shape[1]
    tq = Q_BLOCK
    nq = seq // tq
    span = tq + 2 * WINDOW
    nkeys = ctx_len + span
    ncols = WIN_G * tq
    kvh = pl.program_id(1)
    rep = rep_ref[0]

    def widen(c, carry):
        r0 = pl.multiple_of(c * WIN_REP_ROWS, WIN_REP_ROWS)
        kr_ref[pl.ds(r0, WIN_REP_ROWS), :] = _dot(k_ref[0, pl.ds(r0, WIN_REP_ROWS), :], rep).astype(BF16)
        return carry

    lax.fori_loop(0, seq // WIN_REP_ROWS, widen, 0)
    kcr_ref[...] = _dot(kc_ref[0], rep).astype(BF16)
    vt_ref[:, 0:HEAD_DIM, :] = v_ref[0, :, 0]
    vt_ref[:, HEAD_DIM:, :] = jnp.ones((seq // V7X_LANES, V7X_BF16_SUBLANES, V7X_LANES), BF16)
    vct_ref[:, 0:HEAD_DIM, :] = vc_ref[0, :, 0]
    vct_ref[:, HEAD_DIM:, :] = jnp.ones((ctx_len // V7X_LANES, V7X_BF16_SUBLANES, V7X_LANES), BF16)
    lane_group = lax.broadcasted_iota(jnp.int32, (1, WIN_GW), 1) >> 6
    zero = jnp.zeros((), BF16)
    col_group = lax.broadcasted_iota(jnp.int32, (1, ncols), 1) >> 7
    sink = jnp.zeros((1, ncols), F32)
    for g in range(WIN_G):
        sink = jnp.where(col_group == g, sink_ref[kvh * WIN_G + g], sink)

    def first_window_chunk(t):
        return jnp.clip(t - WINDOW // tq, 0, nq - span // tq)

    def scores(t, s_ref):
        t = jnp.minimum(t, nq - 1)
        r0 = pl.multiple_of(t * tq, tq)
        w0 = pl.multiple_of(first_window_chunk(t) * tq, tq)
        edge = jnp.where(t == 0, 0, jnp.where(t == nq - 1, 2, 1))
        q = q_ref[0, pl.ds(r0, tq), :]
        q4 = jnp.concatenate([jnp.where(lane_group == g, q, zero) for g in range(WIN_G)], axis=0)
        s_c = _dot_nt(kcr_ref[...], q4)
        s_w = _dot_nt(kr_ref[pl.ds(w0, span), :], q4) + bias_ref[edge]
        s_ref[0:ctx_len] = s_c
        s_ref[ctx_len:nkeys] = s_w
        m = jnp.maximum(jnp.maximum(jnp.max(s_c, axis=0, keepdims=True),
                                    jnp.max(s_w, axis=0, keepdims=True)), sink)
        s_ref[nkeys:] = jnp.broadcast_to(m, (WIN_MROWS, ncols))

    def attend(t, s_ref):
        m = s_ref[nkeys:nkeys + 1]
        p = jnp.exp((s_ref[0:nkeys] - m).astype(BF16))
        c0 = first_window_chunk(t)
        v_c = jnp.concatenate([vct_ref[i] for i in range(ctx_len // V7X_LANES)], axis=1)
        v_w = jnp.concatenate([vt_ref[c0 + i] for i in range(span // V7X_LANES)], axis=1)
        acc = _dot(v_c, p[0:ctx_len]) + _dot(v_w, p[ctx_len:nkeys])
        o = acc[0:HEAD_DIM] / (acc[HEAD_DIM:HEAD_DIM + 1] + jnp.exp(sink - m))
        o4 = jnp.concatenate([o[:, g * tq:(g + 1) * tq] for g in range(WIN_G)], axis=0)
        o_ref[0, pl.ds(pl.multiple_of(t * tq, tq), tq), :] = o4.T.astype(BF16)

    def quad(i, carry):
        t = 4 * i
        scores(t + 2, s_refs[2])
        scores(t + 3, s_refs[3])
        attend(t, s_refs[0])
        attend(t + 1, s_refs[1])
        scores(t + 4, s_refs[0])
        scores(t + 5, s_refs[1])
        attend(t + 2, s_refs[2])
        attend(t + 3, s_refs[3])
        return carry

    scores(0, s_refs[0])
    scores(1, s_refs[1])
    lax.fori_loop(0, nq // 4, quad, 0)


def _win_attn(wq, wk, wv, wkc, wvc, rep, bias, sink):
    nbatch, seq, _ = wq.shape
    ctx_len = wkc.shape[1]
    assert (seq // Q_BLOCK) % 4 == 0
    span = Q_BLOCK + 2 * WINDOW
    srows, scols = ctx_len + span + WIN_MROWS, WIN_G * Q_BLOCK
    nch, ncc = seq // V7X_LANES, ctx_len // V7X_LANES
    wv = wv.reshape(nbatch, nch, H_WIN_KV, HEAD_DIM, V7X_LANES)
    wvc = wvc.reshape(nbatch, ncc, H_WIN_KV, HEAD_DIM, V7X_LANES)
    per_b = lambda b, j: (b, 0, 0)
    per_bh = lambda b, j: (b, 0, j)
    vrows = HEAD_DIM + V7X_BF16_SUBLANES
    return pl.pallas_call(
        _win_attn_kernel,
        out_shape=jax.ShapeDtypeStruct((nbatch, seq, WIN_Q), BF16),
        grid=(nbatch, H_WIN_KV),
        in_specs=[pl.BlockSpec((1, seq, WIN_GW), per_bh),
                  pl.BlockSpec((1, seq, WIN_KV), per_b),
                  pl.BlockSpec((1, nch, 1, HEAD_DIM, V7X_LANES), lambda b, j: (b, 0, j, 0, 0)),
                  pl.BlockSpec((1, ctx_len, WIN_KV), per_b),
                  pl.BlockSpec((1, ncc, 1, HEAD_DIM, V7X_LANES), lambda b, j: (b, 0, j, 0, 0)),
                  pl.BlockSpec((1, WIN_KV, WIN_GW), lambda b, j: (j, 0, 0)),
                  _const_spec(bias.shape),
                  pl.BlockSpec(memory_space=pltpu.SMEM)],
        out_specs=pl.BlockSpec((1, seq, WIN_GW), per_bh),
        scratch_shapes=[pltpu.VMEM((seq, WIN_GW), BF16), pltpu.VMEM((ctx_len, WIN_GW), BF16),
                        pltpu.VMEM((nch, vrows, V7X_LANES), BF16),
                        pltpu.VMEM((ncc, vrows, V7X_LANES), BF16),
                        *[pltpu.VMEM((srows, scols), F32) for _ in range(4)]],
        compiler_params=_cparams("arbitrary", "arbitrary"),
        name="window_attention",
    )(wq, wk, wv, wkc, wvc, rep, bias, sink)


def _cos_sin(num, den):
    ang = 2.0 * np.pi * (np.asarray(num, np.int64) % den).astype(np.float64) / den
    return np.cos(ang), np.sin(ang)


def _channel_dft_table():
    k = np.arange(FOURIER_GW)
    c, s = _cos_sin(np.outer(k, k), FOURIER_GW)
    return np.concatenate([c, -s], axis=1)


def _stage_a_table(n1c):
    k = np.arange(n1c)
    c, s = _cos_sin(np.outer(k, k), n1c)
    return np.kron(np.block([[c, s], [-s, c]]), np.eye(FFT_RB))


def _stage_b_table(norm):
    k = np.arange(V7X_LANES)
    c, s = _cos_sin(np.outer(k, k), V7X_LANES)
    kb = np.zeros((V7X_LANES, FFT_RB, 2, FFT_RB, V7X_LANES))
    for i in range(FFT_RB):
        kb[:, i, 0, i, :] = c / norm
        kb[:, i, 1, i, :] = s / norm
    return kb.reshape(V7X_LANES * FFT_RB, 2 * FFT_RB * V7X_LANES)


def _twiddle_tables(seq):
    n1c = seq // V7X_LANES
    c, s = _cos_sin(np.outer(np.arange(n1c), np.arange(V7X_LANES)), seq)
    expand = lambda t: np.repeat(t[:, :, None], V7X_LANES, axis=2)
    return expand(c), expand(s)


def _real_dft_table(n, norm):
    k = np.arange(n)
    c, s = _cos_sin(np.outer(k, k), n)
    return np.concatenate([c, s], axis=1) / norm


def _rope_tables(seq):
    t = np.arange(seq)
    axis_dim = HEAD_DIM // 2
    inv_freq = ROPE_BASE ** (-np.arange(0, axis_dim, 2, dtype=np.float64) / axis_dim)
    ang_r = (t // GRID_W).astype(np.float64)[:, None] * inv_freq[None, :]
    ang_c = (t % GRID_W).astype(np.float64)[:, None] * inv_freq[None, :]
    cos = np.concatenate([np.cos(ang_r)] * 2 + [np.cos(ang_c)] * 2, axis=1)
    sin = np.concatenate([-np.sin(ang_r), np.sin(ang_r), -np.sin(ang_c), np.sin(ang_c)], axis=1)
    reps = V7X_LANES // HEAD_DIM
    return np.tile(cos, (1, reps)), np.tile(sin, (1, reps))


def _window_bias_table():
    span = Q_BLOCK + 2 * WINDOW
    q = np.arange(Q_BLOCK)[None, :]
    k = np.arange(span)[:, None]
    out = []
    for start in (0, WINDOW, 2 * WINDOW):
        vis = np.abs(k - start - q) <= WINDOW
        out.append(np.tile(np.where(vis, 0.0, NEG_INF), (1, WIN_G)))
    return np.stack(out).astype(np.float32)


def _replication_table():
    rep = np.zeros((H_WIN_KV, WIN_KV, WIN_GW), np.float32)
    for j in range(H_WIN_KV):
        for g in range(WIN_G):
            for d in range(HEAD_DIM):
                rep[j, j * HEAD_DIM + d, g * HEAD_DIM + d] = 1.0
    return rep


def _fourier_conv_mix(h2d, mod, gpre, w_in, conv_w, conv_b, ln_g, ln_b, nbatch, tm):
    t = h2d.shape[0]
    seq = t // nbatch
    table = lambda a: jnp.asarray(a, F32).astype(BF16)
    cw = table(_channel_dft_table())
    norm = math.sqrt(seq * FOURIER_GW)
    if seq % (FFT_RB * V7X_LANES) == 0:
        n1c = seq // V7X_LANES
        twc, tws = _twiddle_tables(seq)
        z, u = _ev_fft_a(h2d, mod, gpre, w_in, cw, table(_stage_a_table(n1c)),
                         jnp.asarray(twc, F32), jnp.asarray(tws, F32), nbatch)
        fa = _ev_fft_b(z, table(_stage_b_table(norm)))
        u = u.reshape(t, D_CONV)
    else:
        p, u = _ev_proj(h2d, mod, gpre, w_in, cw, nbatch, tm)
        fa = _dft_real(p.reshape(nbatch, 2 * seq, D_FOURIER), table(_real_dft_table(seq, norm)),
                       D_FOURIER)
    fa = fa.reshape(t, D_FOURIER)
    uc = _conv_ln_swish(u, conv_w, conv_b, ln_g, ln_b, nbatch, tm)
    return fa, uc


def _diff_window_mix(x2d, c2d, modx, modc, gpre, w_in, lam_vec, subln_g, sink, lam_init, nbatch):
    t = x2d.shape[0]
    seq = t // nbatch
    ctx_len = c2d.shape[0] // nbatch
    cos, sin = _rope_tables(seq)
    dq, dk, vt, wq, wk, wv = _od_proj(x2d, modx, gpre, w_in, jnp.asarray(cos, F32),
                                      jnp.asarray(sin, F32), nbatch, 2 * KV_CHUNK)
    dkc, vtc, wkc, wvc = _od_proj_ctx(c2d, modc, gpre, w_in, nbatch)
    b3 = lambda a, n: a.reshape(nbatch, n, a.shape[-1])
    diff = _diff_attn(b3(dq, seq), b3(dk, seq), vt, b3(dkc, ctx_len), vtc, lam_vec, subln_g, lam_init)
    win = _win_attn(b3(wq, seq), b3(wk, seq), wv, b3(wkc, ctx_len), wvc,
                    jnp.asarray(_replication_table(), F32).astype(BF16),
                    jnp.asarray(_window_bias_table(), F32), sink)
    return diff.reshape(t, DIFF_V), win.reshape(t, WIN_Q)


def kernel(x, c, ctx, c_ctx, w_mod, b_mod, norm_pre, norm_post, ffn_w_in, ffn_w_out, ev_w_in, ev_conv_w, ev_conv_b, ev_ln_g, ev_ln_b, ev_w_out, od_w_in, od_lambda, od_subln_g, od_sink, od_w_out):
    nbatch, seq, d = x.shape
    ctx_len = ctx.shape[1]
    depth = w_mod.shape[0]
    assert d == D_MODEL and depth == DEPTH == 2 and ctx_len == KV_CHUNK
    assert seq % FFN_TM == 0 and seq % (8 * V7X_LANES) == 0

    mod_rows = 8
    cs = jnp.concatenate([c, c_ctx[None, :], jnp.zeros((mod_rows - nbatch - 1, d), F32)], axis=0)
    mod = _modulation(cs, w_mod, b_mod).reshape(depth, mod_rows, N_MOD, d)
    row = lambda v: v.reshape(1, -1)

    x2d = x.reshape(nbatch * seq, d)
    c2d = ctx.reshape(nbatch * ctx_len, d)
    ffn_in = ffn_w_in.astype(BF16)
    ffn_out = ffn_w_out.astype(BF16)

    def ffn_x(x2d, l, s, k):
        return _ffn(x2d, mod[l, :nbatch], row(norm_pre[l, s]), row(norm_post[l, s]),
                    ffn_in, ffn_out, (l, k), s, FFN_TM, seq // FFN_TM)

    def ffn_c(c2d, l, s, k):
        return _ffn(c2d, mod[l, nbatch:nbatch + 1], row(norm_pre[l, s]), row(norm_post[l, s]),
                    ffn_in, ffn_out, (l, k), s, ctx_len, 1)

    def mix_ffn(branches, w_mix, h2d, l, m, tm, tpb):
        return _mix_ffn(*branches, w_mix, row(norm_post[l, 1]), h2d, m, row(norm_pre[l, 2]),
                        row(norm_post[l, 2]), ffn_in, ffn_out, (l, 1), 2, tm, tpb)

    l = 0
    modx, modc = mod[l, :nbatch], mod[l, nbatch:nbatch + 1]
    x2d = ffn_x(x2d, l, 0, 0)
    c2d = ffn_c(c2d, l, 0, 0)
    ev_args = (row(norm_pre[l, 1]), ev_w_in[0].astype(BF16), ev_conv_w[0],
               row(ev_conv_b[0]), row(ev_ln_g[0]), row(ev_ln_b[0]))
    ev_out = ev_w_out[0].astype(BF16)
    x2d = mix_ffn(_fourier_conv_mix(x2d, modx, *ev_args, nbatch, ROW_TM), ev_out, x2d, l, modx,
                  FFN_TM, seq // FFN_TM)
    c2d = mix_ffn(_fourier_conv_mix(c2d, modc, *ev_args, nbatch, ctx_len), ev_out, c2d, l, modc,
                  ctx_len, 1)

    l = 1
    modx, modc = mod[l, :nbatch], mod[l, nbatch:nbatch + 1]
    x2d = ffn_x(x2d, l, 0, 0)
    c2d = ffn_c(c2d, l, 0, 0)
    lam_init = 0.8 - 0.6 * math.exp(-0.3 * l)
    branches = _diff_window_mix(x2d, c2d, modx, modc, row(norm_pre[l, 1]), od_w_in[0].astype(BF16),
                                od_lambda[0], row(od_subln_g[0]), od_sink[0], lam_init, nbatch)
    x2d = mix_ffn(branches, od_w_out[0].astype(BF16), x2d, l, modx, FFN_TM, seq // FFN_TM)
    return x2d.reshape(nbatch, seq, d)
```
